```python
import jax, jax.numpy as jnp
from jax import lax
import numpy as np

D_MODEL = 2048
BATCH = 8
SEQ = 2048
DEPTH = 2
DEC_BATCH = 128
DEC_SEQ = 4
PAST_LEN = 8192
PAGE_SIZE = 128

HEAD_DIM = 128
N_HEADS_A = 8
N_HEADS_B = 8
N_KV_B = 2
BAND = 128
DIL_GROUPS = ((128, 1), (512, 4), (2048, 16))
WIN_A = 2048
WIN_B = 128
W_A = N_HEADS_A * HEAD_DIM
W_B = N_HEADS_B * HEAD_DIM
KVW_B = N_KV_B * HEAD_DIM
IN_WIDTH = 3 * W_A + W_B + 2 * KVW_B + 2 * D_MODEL
D_FF = ((8 * D_MODEL + 3 * 256 - 1) // (3 * 256)) * 256
EPS = 1e-6
NEG = -1e30
SCALE = HEAD_DIM ** -0.5

kernel_name = "hybrid_dilated_swa_sink_decoder_step"


def rmsnorm(x, g):
    xf = x.astype(jnp.float32)
    return xf * lax.rsqrt(jnp.mean(xf * xf, axis=-1, keepdims=True) + EPS) * g.astype(jnp.float32)


def alibi_slopes(n):
    return 2.0 ** (-8.0 * jnp.arange(1, n + 1, dtype=jnp.float32) / n)


def project(x, norm_g, w_in, qn_a, kn_a, qn_b, kn_b):
    b, t, _ = x.shape
    h = rmsnorm(x, norm_g)
    p = jnp.einsum('btd,de->bte', h, w_in.astype(jnp.float32))
    splits = [W_A, 2 * W_A, 3 * W_A, 3 * W_A + W_B, 3 * W_A + W_B + KVW_B,
              3 * W_A + W_B + 2 * KVW_B, 3 * W_A + W_B + 2 * KVW_B + D_MODEL]
    qa, ka, va, qb, kb, vb, ga, gb = jnp.split(p, splits, axis=-1)
    qa = rmsnorm(qa.reshape(b, t, N_HEADS_A, HEAD_DIM), qn_a)
    ka = rmsnorm(ka.reshape(b, t, N_HEADS_A, HEAD_DIM), kn_a)
    va = va.reshape(b, t, N_HEADS_A, HEAD_DIM)
    qb = rmsnorm(qb.reshape(b, t, N_HEADS_B, HEAD_DIM), qn_b)
    kb = rmsnorm(kb.reshape(b, t, N_KV_B, HEAD_DIM), kn_b)
    vb = vb.reshape(b, t, N_KV_B, HEAD_DIM)
    return qa, ka, va, qb, kb, vb, h, ga, gb


def band_partial(q, k, v, dil, slopes):
    b, t, hq, e = q.shape
    hk = k.shape[2]
    g = hq // hk
    span = dil * BAND
    pad = (-t) % span
    nb = (t + pad) // span

    def fold(a):
        a = jnp.pad(a.astype(jnp.float32), ((0, 0), (pad, 0), (0, 0), (0, 0)))
        return a.reshape(b, nb, BAND, dil, a.shape[2], e)

    def band(a):
        prev = jnp.pad(a[:, :-1], ((0, 0), (1, 0), (0, 0), (0, 0), (0, 0), (0, 0)))
        return jnp.concatenate([prev, a], axis=2)

    qf = fold(q).reshape(b, nb, BAND, dil, hk, g, e)
    kband = band(fold(k))
    vband = band(fold(v))
    logits = jnp.einsum('bnqrhge,bnsrhe->bnrhgqs', qf, kband) * SCALE
    delta = (jnp.arange(BAND)[:, None] + BAND) - jnp.arange(2 * BAND)[None, :]
    in_band = (delta >= 0) & (delta <= BAND)
    key_pos = ((jnp.arange(nb)[:, None, None] * BAND + jnp.arange(2 * BAND)[None, None, :] - BAND) * dil
               + jnp.arange(dil)[None, :, None])
    real = key_pos >= pad
    mask = in_band[None, None, None, None, None] & real[None, :, :, None, None, None, :]
    bias = -slopes.reshape(hk, g)[:, :, None, None] * (dil * delta).astype(jnp.float32)
    logits = jnp.where(mask, logits + bias, NEG)
    m = logits.max(axis=-1)
    p = jnp.exp(logits - m[..., None])
    l = p.sum(axis=-1)
    o = jnp.einsum('bnrhgqs,bnsrhe->bnqrhge', p, vband)
    m = m.transpose(0, 1, 5, 2, 3, 4).reshape(b, nb * span, hq)[:, pad:]
    l = l.transpose(0, 1, 5, 2, 3, 4).reshape(b, nb * span, hq)[:, pad:]
    o = o.reshape(b, nb * span, hq, e)[:, pad:]
    return m, l, o


def gather_partial(q, k_all, v_all, dil, slopes):
    b, t, hq, e = q.shape
    hk = k_all.shape[2]
    g = hq // hk
    past = k_all.shape[1] - t
    idx = past + jnp.arange(t)[:, None] - dil * jnp.arange(BAND + 1)[None, :]
    valid = idx >= 0
    idx = jnp.maximum(idx, 0)
    kg = jnp.take(k_all.astype(jnp.float32), idx, axis=1)
    vg = jnp.take(v_all.astype(jnp.float32), idx, axis=1)
    qg = q.astype(jnp.float32).reshape(b, t, hk, g, e)
    logits = jnp.einsum('bthge,btjhe->bthgj', qg, kg) * SCALE
    bias = -slopes.reshape(hk, g)[:, :, None] * (dil * jnp.arange(BAND + 1)).astype(jnp.float32)
    logits = jnp.where(valid[None, :, None, None, :], logits + bias, NEG)
    m = logits.max(axis=-1)
    p = jnp.exp(logits - m[..., None])
    l = p.sum(axis=-1)
    o = jnp.einsum('bthgj,btjhe->bthge', p, vg).reshape(b, t, hq, e)
    return m.reshape(b, t, hq), l.reshape(b, t, hq), o


def merge_dilations(parts):
    ms = jnp.stack([p[0] for p in parts])
    ls = jnp.stack([p[1] for p in parts])
    os_ = jnp.stack([p[2] for p in parts])
    m = ms.max(axis=0)
    w = jnp.exp(ms - m)
    return (w[..., None] * os_).sum(axis=0) / (w * ls).sum(axis=0)[..., None]


def apply_sinks(part, sink):
    m, l, o = part
    s = sink.astype(jnp.float32)
    mm = jnp.maximum(m, s)
    a = jnp.exp(m - mm)
    return o * a[..., None] / (l * a + jnp.exp(s - mm))[..., None]


def finish(x, ya, yb, h, ga, gb, w_ba, w_bb, w_o, norm_f, w_gate, w_up, w_down):
    b, t, _ = x.shape
    ya_d = jnp.einsum('bte,ed->btd', ya.reshape(b, t, W_A), w_ba.astype(jnp.float32))
    yb_d = jnp.einsum('bte,ed->btd', yb.reshape(b, t, W_B), w_bb.astype(jnp.float32))
    merged = jax.nn.sigmoid(ga) * ya_d + jax.nn.sigmoid(gb) * yb_d
    x = x + jnp.einsum('btd,de->bte', merged, w_o.astype(jnp.float32)).astype(x.dtype)
    h2 = rmsnorm(x, norm_f)
    u = jax.nn.silu(h2 @ w_gate.astype(jnp.float32)) * (h2 @ w_up.astype(jnp.float32))
    return x + (u @ w_down.astype(jnp.float32)).astype(x.dtype)


def setup_inputs(seed: int = 0) -> dict:
    key = jax.random.key(seed)
    ks = jax.random.split(key, 20)
    f32 = jnp.float32
    rows_a = min(WIN_A, PAST_LEN)
    rows_b = min(WIN_B, PAST_LEN)

    def nrm(k, shape, scale):
        return jax.random.normal(k, shape, f32) * scale

    return {
        'x_prompt': nrm(ks[0], (BATCH, SEQ, D_MODEL), 1.0),
        'x_sample': nrm(ks[1], (DEC_BATCH, DEC_SEQ, D_MODEL), 1.0),
        'cache_a_k': nrm(ks[2], (DEPTH, DEC_BATCH, rows_a, N_HEADS_A, HEAD_DIM), 1.0),
        'cache_a_v': nrm(ks[3], (DEPTH, DEC_BATCH, rows_a, N_HEADS_A, HEAD_DIM), 1.0),
        'cache_b_k': nrm(ks[4], (DEPTH, DEC_BATCH, rows_b, N_KV_B, HEAD_DIM), 1.0),
        'cache_b_v': nrm(ks[5], (DEPTH, DEC_BATCH, rows_b, N_KV_B, HEAD_DIM), 1.0),
        'norm_mix': 1.0 + nrm(ks[6], (DEPTH, D_MODEL), 0.02),
        'w_in': nrm(ks[7], (DEPTH, D_MODEL, IN_WIDTH), D_MODEL ** -0.5),
        'qnorm_a': 1.0 + nrm(ks[8], (DEPTH, HEAD_DIM), 0.02),
        'knorm_a': 1.0 + nrm(ks[9], (DEPTH, HEAD_DIM), 0.02),
        'qnorm_b': 1.0 + nrm(ks[10], (DEPTH, HEAD_DIM), 0.02),
        'knorm_b': 1.0 + nrm(ks[11], (DEPTH, HEAD_DIM), 0.02),
        'sinks_b': nrm(ks[12], (DEPTH, N_HEADS_B), 0.5),
        'w_branch_a': nrm(ks[13], (DEPTH, W_A, D_MODEL), W_A ** -0.5),
        'w_branch_b': nrm(ks[14], (DEPTH, W_B, D_MODEL), W_B ** -0.5),
        'w_out': nrm(ks[15], (DEPTH, D_MODEL, D_MODEL), D_MODEL ** -0.5),
        'norm_ffn': 1.0 + nrm(ks[16], (DEPTH, D_MODEL), 0.02),
        'w_ffn_gate': nrm(ks[17], (DEPTH, D_MODEL, D_FF), D_MODEL ** -0.5),
        'w_ffn_up': nrm(ks[18], (DEPTH, D_MODEL, D_FF), D_MODEL ** -0.5),
        'w_ffn_down': nrm(ks[19], (DEPTH, D_FF, D_MODEL), D_FF ** -0.5),
    }


def reference(x_prompt, x_sample, cache_a_k, cache_a_v, cache_b_k, cache_b_v,
              norm_mix, w_in, qnorm_a, knorm_a, qnorm_b, knorm_b, sinks_b,
              w_branch_a, w_branch_b, w_out, norm_ffn, w_ffn_gate, w_ffn_up, w_ffn_down):
    slopes_a = alibi_slopes(N_HEADS_A)
    slopes_b = alibi_slopes(N_HEADS_B)
    keep_a = min(WIN_A, x_prompt.shape[1])
    keep_b = min(WIN_B, x_prompt.shape[1])
    xp, xs = x_prompt, x_sample
    pak, pav, pbk, pbv = [], [], [], []
    sak, sav, sbk, sbv = [], [], [], []
    for l in range(DEPTH):
        qa, ka, va, qb, kb, vb, h, ga, gb = project(xp, norm_mix[l], w_in[l], qnorm_a[l], knorm_a[l],
                                                    qnorm_b[l], knorm_b[l])
        ya = merge_dilations([band_partial(qa, ka, va, dil, slopes_a) for _, dil in DIL_GROUPS])
        yb = apply_sinks(band_partial(qb, kb, vb, 1, slopes_b), sinks_b[l])
        pak.append(ka[:, -keep_a:])
        pav.append(va[:, -keep_a:])
        pbk.append(kb[:, -keep_b:])
        pbv.append(vb[:, -keep_b:])
        xp = finish(xp, ya, yb, h, ga, gb, w_branch_a[l], w_branch_b[l], w_out[l], norm_ffn[l],
                    w_ffn_gate[l], w_ffn_up[l], w_ffn_down[l])

        qa, ka, va, qb, kb, vb, h, ga, gb = project(xs, norm_mix[l], w_in[l], qnorm_a[l], knorm_a[l],
                                                    qnorm_b[l], knorm_b[l])
        ka_all = jnp.concatenate([cache_a_k[l].astype(jnp.float32), ka], axis=1)
        va_all = jnp.concatenate([cache_a_v[l].astype(jnp.float32), va], axis=1)
        kb_all = jnp.concatenate([cache_b_k[l].astype(jnp.float32), kb], axis=1)
        vb_all = jnp.concatenate([cache_b_v[l].astype(jnp.float32), vb], axis=1)
        ya = merge_dilations([gather_partial(qa, ka_all, va_all, dil, slopes_a) for _, dil in DIL_GROUPS])
        yb = apply_sinks(gather_partial(qb, kb_all, vb_all, 1, slopes_b), sinks_b[l])
        sak.append(ka)
        sav.append(va)
        sbk.append(kb)
        sbv.append(vb)
        xs = finish(xs, ya, yb, h, ga, gb, w_branch_a[l], w_branch_b[l], w_out[l], norm_ffn[l],
                    w_ffn_gate[l], w_ffn_up[l], w_ffn_down[l])
    return (xp, xs,
            jnp.stack(pak), jnp.stack(pav), jnp.stack(pbk), jnp.stack(pbv),
            jnp.stack(sak), jnp.stack(sav), jnp.stack(sbk), jnp.stack(sbv))
```

```python
import functools

import jax
import jax.numpy as jnp
from jax import lax
from jax.experimental import pallas as pl
from jax.experimental.pallas import tpu as pltpu

F32 = jnp.float32
BF16 = jnp.bfloat16

HEAD_DIM = 128
N_HEADS = 8
N_KV_B = 2
BAND = 128
DILATIONS = (1, 4, 16)
EPS = 1e-6
NEG = -1e30
SCALE = HEAD_DIM ** -0.5
LANES = 128
VMEM_LIMIT_BYTES = 56 * 1024 * 1024

TN_IN = 512


def _slope(h):
    return 2.0 ** (-(h + 1))


def _params(*sem):
    return pltpu.CompilerParams(dimension_semantics=sem, vmem_limit_bytes=VMEM_LIMIT_BYTES)


def _rms_rows(x, gain):
    ms = jnp.mean(x * x, axis=-1, keepdims=True)
    return x * lax.rsqrt(ms + EPS) * gain


def _head_rmsnorm(p, gain):
    outs = [_rms_rows(p[:, c * HEAD_DIM:(c + 1) * HEAD_DIM], gain) for c in range(p.shape[1] // HEAD_DIM)]
    return outs[0] if len(outs) == 1 else jnp.concatenate(outs, axis=1)


def _inproj_kernel(x_ref, g_ref, w_ref, qna_ref, kna_ref, qnb_ref, knb_ref,
                   qa_ref, ka_ref, va_ref, qb_ref, kb_ref, vb_ref, ga_ref, gb_ref, h_ref, *, seg):
    j = pl.program_id(1)

    @pl.when(j == 0)
    def _():
        h_ref[...] = _rms_rows(x_ref[...], g_ref[...]).astype(BF16)

    p = jnp.dot(h_ref[...], w_ref[...], preferred_element_type=F32)
    s_qa, s_ka, s_va, s_qb, s_kv, s_ga, s_gb = seg

    @pl.when(j < s_ka)
    def _():
        qa_ref[...] = _head_rmsnorm(p, qna_ref[...] * SCALE).astype(BF16)

    @pl.when((j >= s_ka) & (j < s_va))
    def _():
        ka_ref[...] = _head_rmsnorm(p, kna_ref[...])

    @pl.when((j >= s_va) & (j < s_qb))
    def _():
        va_ref[...] = p

    @pl.when((j >= s_qb) & (j < s_kv))
    def _():
        qb_ref[...] = _head_rmsnorm(p, qnb_ref[...] * SCALE).astype(BF16)

    @pl.when(j == s_kv)
    def _():
        kvw = N_KV_B * HEAD_DIM
        kb_ref[...] = _head_rmsnorm(p[:, :kvw], knb_ref[...])
        vb_ref[...] = p[:, kvw:]

    @pl.when((j >= s_ga) & (j < s_gb))
    def _():
        ga_ref[...] = jax.nn.sigmoid(p).astype(BF16)

    @pl.when(j >= s_gb)
    def _():
        gb_ref[...] = jax.nn.sigmoid(p).astype(BF16)


def _inproj(x, g, w, qna, kna, qnb, knb, tm):
    m, d = x.shape
    wa = N_HEADS * HEAD_DIM
    kvw = N_KV_B * HEAD_DIM
    assert 2 * kvw == TN_IN and wa % TN_IN == 0 and d % TN_IN == 0 and m % tm == 0
    na, ng = wa // TN_IN, d // TN_IN
    seg = (0, na, 2 * na, 3 * na, 4 * na, 4 * na + 1, 4 * na + 1 + ng)
    nj = seg[-1] + ng
    assert w.shape == (d, nj * TN_IN)

    def seg_spec(start, n, width=TN_IN):
        return pl.BlockSpec((tm, width), lambda i, j: (i, jnp.clip(j - start, 0, n - 1)))

    row = lambda n: pl.BlockSpec((1, n), lambda i, j: (0, 0))
    return pl.pallas_call(
        functools.partial(_inproj_kernel, seg=seg),
        grid=(m // tm, nj),
        in_specs=[pl.BlockSpec((tm, d), lambda i, j: (i, 0)), row(d),
                  pl.BlockSpec((d, TN_IN), lambda i, j: (0, j)),
                  row(HEAD_DIM), row(HEAD_DIM), row(HEAD_DIM), row(HEAD_DIM)],
        out_specs=[seg_spec(seg[0], na), seg_spec(seg[1], na), seg_spec(seg[2], na), seg_spec(seg[3], na),
                   seg_spec(seg[4], 1, kvw), seg_spec(seg[4], 1, kvw),
                   seg_spec(seg[5], ng), seg_spec(seg[6], ng)],
        out_shape=[jax.ShapeDtypeStruct((m, wa), BF16), jax.ShapeDtypeStruct((m, wa), F32),
                   jax.ShapeDtypeStruct((m, wa), F32), jax.ShapeDtypeStruct((m, wa), BF16),
                   jax.ShapeDtypeStruct((m, kvw), F32), jax.ShapeDtypeStruct((m, kvw), F32),
                   jax.ShapeDtypeStruct((m, d), BF16), jax.ShapeDtypeStruct((m, d), BF16)],
        scratch_shapes=[pltpu.VMEM((tm, d), BF16)],
        compiler_params=_params("parallel", "arbitrary"),
        name="inproj",
    )(x, g, w, qna, kna, qnb, knb)


def _band_kernel(*refs, n_kv, lc, dil, sink):
    if sink:
        sink_ref, q_ref, kh_ref, k_ref, vh_ref, v_ref, o_ref, kbuf, vbuf = refs
    else:
        q_ref, kh_ref, k_ref, vh_ref, v_ref, o_ref, lse_ref, kbuf, vbuf = refs
    c = pl.program_id(2)
    kbuf[0:BAND, :] = kh_ref[...].astype(BF16)
    kbuf[BAND:, :] = k_ref[...].astype(BF16)
    vbuf[0:BAND, :] = vh_ref[...].astype(BF16)
    vbuf[BAND:, :] = v_ref[...].astype(BF16)

    rows = lax.broadcasted_iota(jnp.int32, (BAND, 2 * BAND), 0)
    cols = lax.broadcasted_iota(jnp.int32, (BAND, 2 * BAND), 1)
    back = rows + BAND - cols
    in_band = (back >= 0) & (back <= BAND)
    backf = back.astype(F32)
    lane = lax.broadcasted_iota(jnp.int32, (BAND, LANES), 1)
    group = N_HEADS // n_kv

    def body(i, carry):
        r0 = pl.multiple_of(i * BAND, BAND)
        first_col = jnp.where((c == 0) & (i == 0), BAND, 0)
        mask = in_band & (cols >= first_col)
        lse_tile = jnp.zeros((BAND, LANES), F32)
        for h in range(N_HEADS):
            hk = h // group
            q = q_ref[pl.ds(r0, BAND), h * HEAD_DIM:(h + 1) * HEAD_DIM]
            kc = kbuf[pl.ds(r0, 2 * BAND), hk * HEAD_DIM:(hk + 1) * HEAD_DIM]
            vc = vbuf[pl.ds(r0, 2 * BAND), hk * HEAD_DIM:(hk + 1) * HEAD_DIM]
            s = lax.dot_general(q, kc, (((1,), (1,)), ((), ())), preferred_element_type=F32)
            s = jnp.where(mask, s - (_slope(h) * dil) * backf, NEG)
            m = jnp.max(s, axis=-1, keepdims=True)
            p = jnp.exp(s - m)
            l = jnp.sum(p, axis=-1, keepdims=True)
            o = jnp.dot(p.astype(BF16), vc, preferred_element_type=F32)
            if sink:
                sk = sink_ref[h]
                mm = jnp.maximum(m, sk)
                a = jnp.exp(m - mm)
                y = o * a / (l * a + jnp.exp(sk - mm))
            else:
                y = o / l
                lse_tile = jnp.where(lane == h, m + jnp.log(l), lse_tile)
            o_ref[pl.ds(r0, BAND), h * HEAD_DIM:(h + 1) * HEAD_DIM] = y.astype(BF16)
        if not sink:
            lse_ref[pl.ds(r0, BAND), :] = lse_tile
        return carry

    lax.fori_loop(0, lc // BAND, body, 0)


def _band_attention(q, k, v, dil, sinks=None):
    b, t, wq = q.shape
    wk = k.shape[2]
    n_kv = wk // HEAD_DIM
    tl = t // dil
    lc = min(tl, 4 * BAND)
    nc = tl // lc
    assert t % dil == 0 and tl % lc == 0 and lc % BAND == 0
    qv = q.reshape(b, tl, dil * wq)
    kv = k.reshape(b, tl, dil * wk)
    vv = v.reshape(b, tl, dil * wk)
    chunk = lambda w: pl.BlockSpec((None, lc, w), lambda bi, r, c: (bi, c, r))
    halo = pl.BlockSpec((None, BAND, wk), lambda bi, r, c: (bi, jnp.maximum(c * (lc // BAND) - 1, 0), r))
    in_specs = [chunk(wq), halo, chunk(wk), halo, chunk(wk)]
    args = [qv, kv, kv, vv, vv]
    out_specs = [chunk(wq)]
    out_shape = [jax.ShapeDtypeStruct((b, tl, dil * wq), BF16)]
    if sinks is None:
        out_specs.append(chunk(LANES))
        out_shape.append(jax.ShapeDtypeStruct((b, tl, dil * LANES), F32))
    else:
        in_specs.insert(0, pl.BlockSpec(memory_space=pltpu.SMEM))
        args.insert(0, sinks)
    outs = pl.pallas_call(
        functools.partial(_band_kernel, n_kv=n_kv, lc=lc, dil=dil, sink=sinks is not None),
        grid=(b, dil, nc),
        in_specs=in_specs, out_specs=out_specs, out_shape=out_shape,
        scratch_shapes=[pltpu.VMEM((lc + BAND, wk), BF16), pltpu.VMEM((lc + BAND, wk), BF16)],
        compiler_params=_params("parallel", "parallel", "arbitrary"),
        name=f"band_d{dil}" + ("_sink" if sinks is not None else ""),
    )(*args)
    if sinks is None:
        return outs[0].reshape(b, t, wq), outs[1].reshape(b, t, LANES)
    return outs[0].reshape(b, t, wq)


def _block_diag_q(q, n_kv):
    group = N_HEADS // n_kv
    zero = jnp.zeros((8, HEAD_DIM), F32)
    blocks = []
    for h in range(N_HEADS):
        piece = q[:, h * HEAD_DIM:(h + 1) * HEAD_DIM]
        cols = [piece if g == h // group else zero for g in range(n_kv)]
        blocks.append(cols[0] if n_kv == 1 else jnp.concatenate(cols, axis=1))
    return jnp.concatenate(blocks, axis=0).astype(BF16)


def _key_tables(past, n_keys, groups):
    rows = lax.broadcasted_iota(jnp.int32, (N_HEADS * 8, n_keys), 0)
    cols = lax.broadcasted_iota(jnp.int32, (N_HEADS * 8, n_keys), 1)
    dist = past + (rows & 7) - cols
    cnt = jnp.zeros((N_HEADS * 8, n_keys), F32)
    for win, dil in groups:
        ok = (dist >= 0) & (dist <= win) & ((dist & (dil - 1)) == 0)
        cnt = cnt + ok.astype(F32)
    slope = jnp.exp2(-((rows >> 3) + 1).astype(F32))
    return cnt, -slope * dist.astype(F32)


def _masked_attention(qbd, kall, vall, cnt, bias):
    s = lax.dot_general(qbd, kall, (((1,), (1,)), ((), ())), preferred_element_type=F32)
    s = jnp.where(cnt > 0.0, s + bias, NEG)
    m = jnp.max(s, axis=-1, keepdims=True)
    p = cnt * jnp.exp(s - m)
    l = jnp.sum(p, axis=-1, keepdims=True)
    o = jnp.dot(p.astype(BF16), vall, preferred_element_type=F32)
    return m, l, o


def _sample_kernel(sink_ref, qa_ref, kan_ref, van_ref, cak_ref, cav_ref,
                   qb_ref, kbn_ref, vbn_ref, cbk_ref, cbv_ref,
                   ya_ref, yb_ref,
                   kall_a, vall_a, kall_b, vall_b, cnt_a, bias_a, cnt_b, bias_b, *, past_a, past_b, new_rows):
    @pl.when(pl.program_id(0) == 0)
    def _():
        for buf, past in ((kall_a, past_a), (vall_a, past_a), (kall_b, past_b), (vall_b, past_b)):
            buf[past:, :] = jnp.zeros((buf.shape[0] - past, buf.shape[1]), BF16)
        ca, ba = _key_tables(past_a, kall_a.shape[0], tuple((BAND * d, d) for d in DILATIONS))
        cnt_a[...] = ca
        bias_a[...] = ba
        cb, bb = _key_tables(past_b, kall_b.shape[0], ((BAND, 1),))
        cnt_b[...] = cb
        bias_b[...] = bb

    kall_a[0:past_a, :] = cak_ref[...].astype(BF16)
    kall_a[past_a:past_a + new_rows, :] = kan_ref[...].astype(BF16)
    vall_a[0:past_a, :] = cav_ref[...].astype(BF16)
    vall_a[past_a:past_a + new_rows, :] = van_ref[...].astype(BF16)
    kall_b[0:past_b, :] = cbk_ref[...].astype(BF16)
    kall_b[past_b:past_b + new_rows, :] = kbn_ref[...].astype(BF16)
    vall_b[0:past_b, :] = cbv_ref[...].astype(BF16)
    vall_b[past_b:past_b + new_rows, :] = vbn_ref[...].astype(BF16)

    _, l, o = _masked_attention(_block_diag_q(qa_ref[...], N_HEADS), kall_a[...], vall_a[...],
                                cnt_a[...], bias_a[...])
    ya = [o[h * 8:(h + 1) * 8, h * HEAD_DIM:(h + 1) * HEAD_DIM] / l[h * 8:(h + 1) * 8] for h in range(N_HEADS)]
    ya_ref[...] = jnp.concatenate(ya, axis=1)

    m, l, o = _masked_attention(_block_diag_q(qb_ref[...], N_KV_B), kall_b[...], vall_b[...],
                                cnt_b[...], bias_b[...])
    group = N_HEADS // N_KV_B
    yb = []
    for h in range(N_HEADS):
        hk = h // group
        sk = sink_ref[h]
        mh, lh = m[h * 8:(h + 1) * 8], l[h * 8:(h + 1) * 8]
        oh = o[h * 8:(h + 1) * 8, hk * HEAD_DIM:(hk + 1) * HEAD_DIM]
        mm = jnp.maximum(mh, sk)
        a = jnp.exp(mh - mm)
        yb.append(oh * a / (lh * a + jnp.exp(sk - mm)))
    yb_ref[...] = jnp.concatenate(yb, axis=1)


def _sample_attention(sinks, qa, ka_new, va_new, cache_ak, cache_av, qb, kb_new, vb_new, cache_bk, cache_bv):
    b, past_a, wa = cache_ak.shape
    _, past_b, wb = cache_bk.shape
    new_rows = ka_new.shape[1]
    wq = qa.shape[2]
    blk = lambda r, w: pl.BlockSpec((None, r, w), lambda i: (i, 0, 0))
    return pl.pallas_call(
        functools.partial(_sample_kernel, past_a=past_a, past_b=past_b, new_rows=new_rows),
        grid=(b,),
        in_specs=[pl.BlockSpec(memory_space=pltpu.SMEM),
                  blk(8, wq), blk(new_rows, wa), blk(new_rows, wa), blk(past_a, wa), blk(past_a, wa),
                  blk(8, wq), blk(new_rows, wb), blk(new_rows, wb), blk(past_b, wb), blk(past_b, wb)],
        out_specs=[blk(8, wq), blk(8, wq)],
        out_shape=[jax.ShapeDtypeStruct((b, 8, wq), F32), jax.ShapeDtypeStruct((b, 8, wq), F32)],
        scratch_shapes=[pltpu.VMEM((past_a + LANES, wa), BF16), pltpu.VMEM((past_a + LANES, wa), BF16),
                        pltpu.VMEM((past_b + LANES, wb), BF16), pltpu.VMEM((past_b + LANES, wb), BF16),
                        pltpu.VMEM((N_HEADS * 8, past_a + LANES), F32), pltpu.VMEM((N_HEADS * 8, past_a + LANES), F32),
                        pltpu.VMEM((N_HEADS * 8, past_b + LANES), F32), pltpu.VMEM((N_HEADS * 8, past_b + LANES), F32)],
        compiler_params=_params("arbitrary"),
        name="sample_attn",
    )(sinks, qa, ka_new, va_new, cache_ak, cache_av, qb, kb_new, vb_new, cache_bk, cache_bv)


def _post_kernel(*refs, n_groups, tn):
    if n_groups:
        o_refs, refs = refs[:n_groups], refs[n_groups:]
        lse_refs, refs = refs[:n_groups], refs[n_groups:]
    else:
        ya_ref, refs = refs[0], refs[1:]
    yb_ref, ga_ref, gb_ref, x_ref, wba_ref, wbb_ref, wo_ref, out_ref, ya_s, mg_s = refs
    if n_groups:
        lses = [r[...] for r in lse_refs]
        for h in range(N_HEADS):
            e = [l[:, h:h + 1] for l in lses]
            m = functools.reduce(jnp.maximum, e)
            w = [jnp.exp(ei - m) for ei in e]
            sl = slice(h * HEAD_DIM, (h + 1) * HEAD_DIM)
            num = sum(wi * r[:, sl].astype(F32) for wi, r in zip(w, o_refs))
            ya_s[:, sl] = (num / sum(w)).astype(BF16)
    else:
        ya_s[...] = ya_ref[...].astype(BF16)
    d = out_ref.shape[1]
    yb = yb_ref[...].astype(BF16)
    for n in range(d // tn):
        sl = slice(n * tn, (n + 1) * tn)
        ya_d = jnp.dot(ya_s[...], wba_ref[:, sl], preferred_element_type=F32)
        yb_d = jnp.dot(yb, wbb_ref[:, sl], preferred_element_type=F32)
        mg_s[:, sl] = (ga_ref[:, sl].astype(F32) * ya_d + gb_ref[:, sl].astype(F32) * yb_d).astype(BF16)
    for n in range(d // tn):
        sl = slice(n * tn, (n + 1) * tn)
        out_ref[:, sl] = x_ref[:, sl] + jnp.dot(mg_s[...], wo_ref[:, sl], preferred_element_type=F32)


def _post(ya_parts, yb, ga, gb, x, wba, wbb, wo, tm):
    m, d = x.shape
    wa = wba.shape[0]
    rows = lambda w: pl.BlockSpec((tm, w), lambda i: (i, 0))
    whole = lambda a: pl.BlockSpec(a.shape, lambda i: (0, 0), pipeline_mode=pl.Buffered(1))
    if isinstance(ya_parts, tuple):
        os_, lses = ya_parts
        n_groups = len(os_)
        ya_args = list(os_) + list(lses)
        ya_specs = [rows(wa)] * n_groups + [rows(LANES)] * n_groups
    else:
        n_groups = 0
        ya_args = [ya_parts]
        ya_specs = [rows(wa)]
    return pl.pallas_call(
        functools.partial(_post_kernel, n_groups=n_groups, tn=512),
        grid=(m // tm,),
        in_specs=ya_specs + [rows(wa), rows(d), rows(d), rows(d), whole(wba), whole(wbb), whole(wo)],
        out_specs=rows(d),
        out_shape=jax.ShapeDtypeStruct((m, d), F32),
        scratch_shapes=[pltpu.VMEM((tm, wa), BF16), pltpu.VMEM((tm, d), BF16)],
        compiler_params=_params("parallel"),
        name="post_attn",
    )(*ya_args, yb, ga, gb, x, wba, wbb, wo)


def _ffn_kernel(x_ref, g_ref, wg_ref, wu_ref, wd_ref, out_ref, h_ref):
    f = pl.program_id(1)

    @pl.when(f == 0)
    def _():
        x = x_ref[...]
        h_ref[...] = _rms_rows(x, g_ref[...]).astype(BF16)
        out_ref[...] = x

    h = h_ref[...]
    gate = jnp.dot(h, wg_ref[...], preferred_element_type=F32)
    up = jnp.dot(h, wu_ref[...], preferred_element_type=F32)
    u = (jax.nn.silu(gate) * up).astype(BF16)
    out_ref[...] += jnp.dot(u, wd_ref[...], preferred_element_type=F32)


def _ffn(x, g, wg, wu, wd, tm, tf):
    m, d = x.shape
    dff = wg.shape[1]
    assert m % tm == 0 and dff % tf == 0
    return pl.pallas_call(
        _ffn_kernel,
        grid=(m // tm, dff // tf),
        in_specs=[pl.BlockSpec((tm, d), lambda i, f: (i, 0)), pl.BlockSpec((1, d), lambda i, f: (0, 0)),
                  pl.BlockSpec((d, tf), lambda i, f: (0, f)), pl.BlockSpec((d, tf), lambda i, f: (0, f)),
                  pl.BlockSpec((tf, d), lambda i, f: (f, 0))],
        out_specs=pl.BlockSpec((tm, d), lambda i, f: (i, 0)),
        out_shape=jax.ShapeDtypeStruct((m, d), F32),
        scratch_shapes=[pltpu.VMEM((tm, d), BF16)],
        compiler_params=_params("parallel", "arbitrary"),
        name="ffn",
    )(x, g, wg, wu, wd)


def _pad_rows(a, n):
    return jnp.pad(a, ((0, 0), (0, n - a.shape[1]), (0, 0)))


def kernel(x_prompt, x_sample, cache_a_k, cache_a_v, cache_b_k, cache_b_v, norm_mix, w_in, qnorm_a, knorm_a,
           qnorm_b, knorm_b, sinks_b, w_branch_a, w_branch_b, w_out, norm_ffn, w_ffn_gate, w_ffn_up, w_ffn_down):
    depth = w_in.shape[0]
    b, t, d = x_prompt.shape
    sb, st, _ = x_sample.shape
    wa = N_HEADS * HEAD_DIM
    kvw = N_KV_B * HEAD_DIM
    keep_b = min(BAND, t)
    xp = x_prompt.reshape(b * t, d)
    xs = x_sample.reshape(sb * st, d)
    tm_p = 512
    tm_s = sb * st
    row = lambda v: v.reshape(1, -1)
    outs = [[] for _ in range(8)]
    for l in range(depth):
        w_in_l = w_in[l].astype(BF16)
        wba, wbb, wo = w_branch_a[l].astype(BF16), w_branch_b[l].astype(BF16), w_out[l].astype(BF16)
        wg, wu, wd = w_ffn_gate[l].astype(BF16), w_ffn_up[l].astype(BF16), w_ffn_down[l].astype(BF16)
        norms = (row(qnorm_a[l]), row(knorm_a[l]), row(qnorm_b[l]), row(knorm_b[l]))

        qa, ka, va, qb, kb, vb, ga, gb = _inproj(xp, row(norm_mix[l]), w_in_l, *norms, tm=tm_p)
        seq = lambda a: a.reshape(b, t, a.shape[1])
        parts = [_band_attention(seq(qa), seq(ka), seq(va), dil) for dil in DILATIONS]
        os_ = [o.reshape(b * t, wa) for o, _ in parts]
        lses = [s.reshape(b * t, LANES) for _, s in parts]
        yb = _band_attention(seq(qb), seq(kb), seq(vb), 1, sinks=sinks_b[l]).reshape(b * t, wa)
        x1 = _post((os_, lses), yb, ga, gb, xp, wba, wbb, wo, tm=tm_p)
        xp = _ffn(x1, row(norm_ffn[l]), wg, wu, wd, tm=tm_p, tf=512)
        outs[0].append(ka.reshape(b, t, N_HEADS, HEAD_DIM))
        outs[1].append(va.reshape(b, t, N_HEADS, HEAD_DIM))
        outs[2].append(kb.reshape(b, t, N_KV_B, HEAD_DIM)[:, t - keep_b:])
        outs[3].append(vb.reshape(b, t, N_KV_B, HEAD_DIM)[:, t - keep_b:])

        qa, ka, va, qb, kb, vb, ga, gb = _inproj(xs, row(norm_mix[l]), w_in_l, *norms, tm=tm_s)
        tok = lambda a: a.reshape(sb, st, a.shape[1])
        ya, yb = _sample_attention(
            sinks_b[l],
            _pad_rows(tok(qa).astype(F32), 8), _pad_rows(tok(ka), 16), _pad_rows(tok(va), 16),
            cache_a_k[l].reshape(sb, -1, wa), cache_a_v[l].reshape(sb, -1, wa),
            _pad_rows(tok(qb).astype(F32), 8), _pad_rows(tok(kb), 16), _pad_rows(tok(vb), 16),
            cache_b_k[l].reshape(sb, -1, kvw), cache_b_v[l].reshape(sb, -1, kvw))
        ya = ya[:, :st].reshape(sb * st, wa)
        yb = yb[:, :st].reshape(sb * st, wa)
        x1 = _post(ya, yb, ga, gb, xs, wba, wbb, wo, tm=tm_s)
        xs = _ffn(x1, row(norm_ffn[l]), wg, wu, wd, tm=tm_s, tf=512)
        outs[4].append(ka.reshape(sb, st, N_HEADS, HEAD_DIM))
        outs[5].append(va.reshape(sb, st, N_HEADS, HEAD_DIM))
        outs[6].append(kb.reshape(sb, st, N_KV_B, HEAD_DIM))
        outs[7].append(vb.reshape(sb, st, N_KV_B, HEAD_DIM))
    return (xp.reshape(b, t, d), xs.reshape(sb, st, d)) + tuple(jnp.stack(o) for o in outs)
```

```python
import functools

import jax
import jax.numpy as jnp
from jax import lax
from jax.experimental import pallas as pl
from jax.experimental.pallas import tpu as pltpu

F32 = jnp.float32
BF16 = jnp.bfloat16

HEAD_DIM = 128
N_HEADS = 8
N_KV_B = 2
BAND = 128
DILATIONS = (1, 4, 16)
MAX_DIL = DILATIONS[-1]
EPS = 1e-6
NEG = -1e30
SCALE = HEAD_DIM ** -0.5
LANES = 128
SUBLANES = 8
BF16_ROWS = 16
VMEM_LIMIT_BYTES = 56 * 1024 * 1024

TN_IN = 1024


def _slope(h):
    return 2.0 ** (-(h + 1))


def _params(*sem):
    return pltpu.CompilerParams(dimension_semantics=sem, vmem_limit_bytes=VMEM_LIMIT_BYTES)


def _rms_rows(x, gain):
    ms = jnp.mean(x * x, axis=-1, keepdims=True)
    return x * lax.rsqrt(ms + EPS) * gain


def _head_cols(h):
    return slice(h * HEAD_DIM, (h + 1) * HEAD_DIM)


def _heads_to_native(parts):
    return jnp.swapaxes(jnp.stack(parts, axis=0), 0, 1)


def _native_to_heads(x):
    return jnp.swapaxes(x, 0, 1)


def _inproj_kernel(x_ref, g_ref, w_ref, qna_ref, kna_ref, qnb_ref, knb_ref,
                   qa_ref, ka_ref, va_ref, qb_ref, kb_ref, vb_ref, gates_ref, h_ref):
    j = pl.program_id(1)

    @pl.when(j == 0)
    def _():
        h_ref[...] = _rms_rows(x_ref[...], g_ref[...]).astype(BF16)

    p = jnp.dot(h_ref[...], w_ref[...], preferred_element_type=F32)

    @pl.when(j == 0)
    def _():
        gain = qna_ref[...] * SCALE
        qa_ref[...] = _heads_to_native([_rms_rows(p[:, _head_cols(h)], gain) for h in range(N_HEADS)])

    @pl.when(j == 1)
    def _():
        ka_ref[...] = _heads_to_native([_rms_rows(p[:, _head_cols(h)], kna_ref[...]) for h in range(N_HEADS)])

    @pl.when(j == 2)
    def _():
        va_ref[...] = _heads_to_native([p[:, _head_cols(h)] for h in range(N_HEADS)])

    @pl.when(j == 3)
    def _():
        gain = qnb_ref[...] * SCALE
        for h in range(N_HEADS):
            qb_ref[:, _head_cols(h)] = _rms_rows(p[:, _head_cols(h)], gain).astype(BF16)

    @pl.when(j == 4)
    def _():
        for h in range(N_KV_B):
            kb_ref[:, _head_cols(h)] = _rms_rows(p[:, _head_cols(h)], knb_ref[...])
        vb_ref[...] = p[:, N_KV_B * HEAD_DIM:2 * N_KV_B * HEAD_DIM]

    @pl.when(j >= 5)
    def _():
        gates_ref[...] = jax.nn.sigmoid(p).astype(BF16)


def _pack_w_in(w, d):
    wa = N_HEADS * HEAD_DIM
    kvw = N_KV_B * HEAD_DIM
    assert w.shape[1] == 4 * wa + 2 * kvw + 2 * d and 2 * kvw <= TN_IN and wa == TN_IN
    pad = jnp.zeros((w.shape[0], TN_IN - 2 * kvw), w.dtype)
    return jnp.concatenate([w[:, :4 * wa + 2 * kvw], pad, w[:, 4 * wa + 2 * kvw:]], axis=1).astype(BF16)


def _inproj(x, g, w, qna, kna, qnb, knb, tm):
    m, d = x.shape
    wa = N_HEADS * HEAD_DIM
    kvw = N_KV_B * HEAD_DIM
    assert m % tm == 0 and d % TN_IN == 0
    ng = 2 * d // TN_IN
    nj = 5 + ng
    assert w.shape == (d, nj * TN_IN)
    native = pl.BlockSpec((tm, N_HEADS, HEAD_DIM), lambda i, j: (i, 0, 0))
    flat = lambda width: pl.BlockSpec((tm, width), lambda i, j: (i, 0))
    row = lambda n: pl.BlockSpec((1, n), lambda i, j: (0, 0))
    nat_shape = jax.ShapeDtypeStruct((m, N_HEADS, HEAD_DIM), F32)
    return pl.pallas_call(
        _inproj_kernel,
        grid=(m // tm, nj),
        in_specs=[pl.BlockSpec((tm, d), lambda i, j: (i, 0)), row(d),
                  pl.BlockSpec((d, TN_IN), lambda i, j: (0, j)),
                  row(HEAD_DIM), row(HEAD_DIM), row(HEAD_DIM), row(HEAD_DIM)],
        out_specs=[native, native, native, flat(wa), flat(kvw), flat(kvw),
                   pl.BlockSpec((tm, TN_IN), lambda i, j: (i, jnp.clip(j - 5, 0, ng - 1)))],
        out_shape=[nat_shape, nat_shape, nat_shape, jax.ShapeDtypeStruct((m, wa), BF16),
                   jax.ShapeDtypeStruct((m, kvw), F32), jax.ShapeDtypeStruct((m, kvw), F32),
                   jax.ShapeDtypeStruct((m, 2 * d), BF16)],
        scratch_shapes=[pltpu.VMEM((tm, d), BF16)],
        compiler_params=_params("parallel", "arbitrary"),
        name="inproj",
    )(x, g, w, qna, kna, qnb, knb)


def _band_kernel(*refs, n_kv, lc, dil, sink, native):
    if sink:
        sink_ref, q_ref, kh_ref, k_ref, vh_ref, v_ref, o_ref, qbuf, kbuf, vbuf = refs
    else:
        q_ref, kh_ref, k_ref, vh_ref, v_ref, o_ref, lse_ref, qbuf, kbuf, vbuf = refs
    c = pl.program_id(2)
    if native:
        heads = lambda ref: _native_to_heads(ref[...]).astype(BF16)
        qbuf[...] = heads(q_ref)
        for buf, halo_ref, ref in ((kbuf, kh_ref, k_ref), (vbuf, vh_ref, v_ref)):
            buf[:, 0:BAND, :] = heads(halo_ref)
            buf[:, BAND:, :] = heads(ref)
    else:
        for h in range(N_HEADS):
            qbuf[h] = q_ref[:, _head_cols(h)]
        for buf, halo_ref, ref in ((kbuf, kh_ref, k_ref), (vbuf, vh_ref, v_ref)):
            for h in range(n_kv):
                buf[h, 0:BAND, :] = halo_ref[:, _head_cols(h)].astype(BF16)
                buf[h, BAND:, :] = ref[:, _head_cols(h)].astype(BF16)

    rows = lax.broadcasted_iota(jnp.int32, (BAND, 2 * BAND), 0)
    cols = lax.broadcasted_iota(jnp.int32, (BAND, 2 * BAND), 1)
    back = rows + BAND - cols
    in_band = (back >= 0) & (back <= BAND)
    backf = back.astype(F32)
    lane = lax.broadcasted_iota(jnp.int32, (BAND, LANES), 1)
    group = N_HEADS // n_kv

    def body(i, carry):
        r0 = pl.multiple_of(i * BAND, BAND)
        first_col = jnp.where((c == 0) & (i == 0), BAND, 0)
        mask = in_band & (cols >= first_col)
        lse_tile = jnp.zeros((BAND, LANES), F32)
        ys = []
        for h in range(N_HEADS):
            hk = h // group
            q = qbuf[h, pl.ds(r0, BAND), :]
            kc = kbuf[hk, pl.ds(r0, 2 * BAND), :]
            vc = vbuf[hk, pl.ds(r0, 2 * BAND), :]
            s = lax.dot_general(q, kc, (((1,), (1,)), ((), ())), preferred_element_type=F32)
            s = jnp.where(mask, s - (_slope(h) * dil) * backf, NEG)
            m = jnp.max(s, axis=-1, keepdims=True)
            p = jnp.exp(s - m)
            l = jnp.sum(p, axis=-1, keepdims=True)
            o = jnp.dot(p.astype(BF16), vc, preferred_element_type=F32)
            if sink:
                sk = sink_ref[h]
                mm = jnp.maximum(m, sk)
                a = jnp.exp(m - mm)
                y = o * a / (l * a + jnp.exp(sk - mm))
            else:
                y = o / l
                lse_tile = jnp.where(lane == h, m + jnp.log(l), lse_tile)
            if native:
                ys.append(y)
            else:
                o_ref[pl.ds(r0, BAND), _head_cols(h)] = y.astype(BF16)
        if native:
            o_ref[pl.ds(r0, BAND)] = _heads_to_native(ys)
        if not sink:
            lse_ref[pl.ds(r0, BAND), :] = lse_tile
        return carry

    lax.fori_loop(0, lc // BAND, body, 0)


def _band_chunks(t, dil):
    tl = t // dil
    lc = min(tl, 4 * BAND)
    assert t % dil == 0 and tl % lc == 0 and lc % BAND == 0
    return tl, lc


def _band_attention_a(q, k, v, dil):
    b, t, nh, e = q.shape
    tl, lc = _band_chunks(t, dil)
    view = lambda a: a.reshape(b, tl, dil, nh, e)
    chunk = pl.BlockSpec((None, lc, None, nh, e), lambda bi, r, c: (bi, c, r, 0, 0))
    halo = pl.BlockSpec((None, BAND, None, nh, e),
                        lambda bi, r, c: (bi, jnp.maximum(c * (lc // BAND) - 1, 0), r, 0, 0))
    o, lse = pl.pallas_call(
        functools.partial(_band_kernel, n_kv=nh, lc=lc, dil=dil, sink=False, native=True),
        grid=(b, dil, tl // lc),
        in_specs=[chunk, halo, chunk, halo, chunk],
        out_specs=[chunk, pl.BlockSpec((None, lc, LANES), lambda bi, r, c: (bi, c, r))],
        out_shape=[jax.ShapeDtypeStruct((b, tl, dil, nh, e), F32),
                   jax.ShapeDtypeStruct((b, tl, dil * LANES), F32)],
        scratch_shapes=[pltpu.VMEM((nh, lc, e), BF16),
                        pltpu.VMEM((nh, lc + BAND, e), BF16), pltpu.VMEM((nh, lc + BAND, e), BF16)],
        compiler_params=_params("parallel", "parallel", "arbitrary"),
        name=f"band_d{dil}",
    )(view(q), view(k), view(k), view(v), view(v))
    return o.reshape(b, t, nh, e), lse.reshape(b, t, LANES)


def _band_attention_b(q, k, v, sinks):
    b, t, wq = q.shape
    wk = k.shape[2]
    tl, lc = _band_chunks(t, 1)
    chunk = lambda w: pl.BlockSpec((None, lc, w), lambda bi, r, c: (bi, c, 0))
    halo = pl.BlockSpec((None, BAND, wk), lambda bi, r, c: (bi, jnp.maximum(c * (lc // BAND) - 1, 0), 0))
    n_kv = wk // HEAD_DIM
    return pl.pallas_call(
        functools.partial(_band_kernel, n_kv=n_kv, lc=lc, dil=1, sink=True, native=False),
        grid=(b, 1, tl // lc),
        in_specs=[pl.BlockSpec(memory_space=pltpu.SMEM), chunk(wq), halo, chunk(wk), halo, chunk(wk)],
        out_specs=chunk(wq),
        out_shape=jax.ShapeDtypeStruct((b, t, wq), BF16),
        scratch_shapes=[pltpu.VMEM((N_HEADS, lc, HEAD_DIM), BF16),
                        pltpu.VMEM((n_kv, lc + BAND, HEAD_DIM), BF16),
                        pltpu.VMEM((n_kv, lc + BAND, HEAD_DIM), BF16)],
        compiler_params=_params("parallel", "parallel", "arbitrary"),
        name="band_sink",
    )(sinks, q, k, k, v, v)


def _sample_key_table(past, st, n_far, n_near, n_keys, dilations):
    rows = lax.broadcasted_iota(jnp.int32, (SUBLANES, n_keys), 0)
    j = lax.broadcasted_iota(jnp.int32, (SUBLANES, n_keys), 1)
    per_res = max(n_far // st, 1)
    far = (j % per_res) * MAX_DIL + j // per_res
    near = past - n_near + (j - n_far)
    n = jnp.where(j < n_far, far, near)
    dist = past + rows - n
    cnt = jnp.zeros((SUBLANES, n_keys), F32)
    for dil in dilations:
        ok = (dist >= 0) & (dist <= BAND * dil) & ((dist & (dil - 1)) == 0)
        cnt = cnt + ok.astype(F32)
    return cnt, dist.astype(F32)


def _softmax_pv(s, cnt, vall):
    m = jnp.max(s, axis=-1, keepdims=True)
    p = cnt * jnp.exp(s - m)
    l = jnp.sum(p, axis=-1, keepdims=True)
    return m, l, jnp.dot(p.astype(BF16), vall, preferred_element_type=F32)


def _sample_kernel(*refs, past_a, past_b, st):
    sink_ref, qa_ref, kan_ref, van_ref = refs[:4]
    kfar_refs, knear_ref = refs[4:4 + st], refs[4 + st]
    vfar_refs, vnear_ref = refs[5 + st:5 + 2 * st], refs[5 + 2 * st]
    (qb_ref, kbn_ref, vbn_ref, cbk_ref, cbv_ref, ya_ref, yb_ref,
     kall_a, vall_a, kall_b, vall_b, cnt_a, dist_a, cnt_b, dist_b) = refs[6 + 2 * st:]
    per_res = kfar_refs[0].shape[0]
    n_far = st * per_res
    n_near = knear_ref.shape[0]
    n_new = kan_ref.shape[0]
    used_a = n_far + n_near + n_new
    used_b = past_b + n_new

    @pl.when(pl.program_id(0) == 0)
    def _():
        for buf, used in ((kall_a, used_a), (vall_a, used_a), (kall_b, used_b), (vall_b, used_b)):
            buf[:, used:, :] = jnp.zeros((buf.shape[0], buf.shape[1] - used, buf.shape[2]), BF16)
        ca, da = _sample_key_table(past_a, st, n_far, n_near, kall_a.shape[1], DILATIONS)
        cnt_a[...] = ca
        dist_a[...] = da
        cb, db = _sample_key_table(past_b, st, 0, past_b, kall_b.shape[1], (1,))
        cnt_b[...] = cb
        dist_b[...] = db

    for far_refs, src_near, src_new, dst in ((kfar_refs, knear_ref, kan_ref, kall_a),
                                             (vfar_refs, vnear_ref, van_ref, vall_a)):
        heads = lambda ref: _native_to_heads(ref[...]).astype(BF16)
        for r, src_far in enumerate(far_refs):
            dst[:, r * per_res:(r + 1) * per_res, :] = heads(src_far)
        dst[:, n_far:n_far + n_near, :] = heads(src_near)
        dst[:, n_far + n_near:used_a, :] = heads(src_new)
    for src_old, src_new, dst in ((cbk_ref, kbn_ref, kall_b), (cbv_ref, vbn_ref, vall_b)):
        for h in range(N_KV_B):
            dst[h, 0:past_b, :] = src_old[:, h, :].astype(BF16)
            dst[h, past_b:used_b, :] = src_new[:, _head_cols(h)].astype(BF16)

    nt = (((1,), (1,)), ((), ()))
    cnt, dist = cnt_a[...], dist_a[...]
    qa = _native_to_heads(qa_ref[...]).astype(BF16)
    for h in range(N_HEADS):
        s = lax.dot_general(qa[h], kall_a[h], nt, preferred_element_type=F32)
        s = jnp.where(cnt > 0.0, s - _slope(h) * dist, NEG)
        _, l, o = _softmax_pv(s, cnt, vall_a[h])
        ya_ref[:, _head_cols(h)] = o / l

    cnt, dist = cnt_b[...], dist_b[...]
    group = N_HEADS // N_KV_B
    for h in range(N_HEADS):
        q = qb_ref[:, _head_cols(h)].astype(BF16)
        s = lax.dot_general(q, kall_b[h // group], nt, preferred_element_type=F32)
        s = jnp.where(cnt > 0.0, s - _slope(h) * dist, NEG)
        m, l, o = _softmax_pv(s, cnt, vall_b[h // group])
        sk = sink_ref[h]
        mm = jnp.maximum(m, sk)
        a = jnp.exp(m - mm)
        yb_ref[:, _head_cols(h)] = o * a / (l * a + jnp.exp(sk - mm))


def _sample_attention(layer, st, sinks, qa, ka_new, va_new, cache_ak, cache_av, qb, kb_new, vb_new,
                      cache_bk, cache_bv):
    depth, b, past_a, nh, e = cache_ak.shape
    past_b, n_kv = cache_bk.shape[2], cache_bk.shape[3]
    n_new = ka_new.shape[1]
    wq = nh * e
    near_rows = BAND * DILATIONS[-2]
    assert past_a == BAND * MAX_DIL and st <= DILATIONS[-2] and past_b >= BAND
    assert past_b % BF16_ROWS == 0 and n_new % BF16_ROWS == 0
    n_far_blocks = (past_a - near_rows) // MAX_DIL
    n_keys_a = pl.cdiv(n_far_blocks * st + near_rows + n_new, LANES) * LANES
    n_keys_b = pl.cdiv(past_b + n_new, LANES) * LANES
    assert n_far_blocks % BF16_ROWS == 0 and near_rows % BF16_ROWS == 0 and past_a % near_rows == 0
    view = lambda c: c.reshape(depth, b, past_a // MAX_DIL, MAX_DIL, nh, e)
    far = [pl.BlockSpec((None, None, n_far_blocks, None, nh, e), lambda i, r=r: (layer, i, 0, r, 0, 0))
           for r in range(st)]
    near = pl.BlockSpec((None, None, near_rows, nh, e), lambda i: (layer, i, past_a // near_rows - 1, 0, 0))
    per_b = lambda *shape: pl.BlockSpec((None,) + shape, lambda i: (i,) + (0,) * len(shape))
    cache_b = pl.BlockSpec((None, None, past_b, n_kv, e), lambda i: (layer, i, 0, 0, 0))
    return pl.pallas_call(
        functools.partial(_sample_kernel, past_a=past_a, past_b=past_b, st=st),
        grid=(b,),
        in_specs=[pl.BlockSpec(memory_space=pltpu.SMEM),
                  per_b(SUBLANES, nh, e), per_b(n_new, nh, e), per_b(n_new, nh, e), *far, near, *far, near,
                  per_b(SUBLANES, wq), per_b(n_new, n_kv * e), per_b(n_new, n_kv * e), cache_b, cache_b],
        out_specs=[per_b(SUBLANES, wq), per_b(SUBLANES, wq)],
        out_shape=[jax.ShapeDtypeStruct((b, SUBLANES, wq), F32), jax.ShapeDtypeStruct((b, SUBLANES, wq), F32)],
        scratch_shapes=[pltpu.VMEM((nh, n_keys_a, e), BF16), pltpu.VMEM((nh, n_keys_a, e), BF16),
                        pltpu.VMEM((n_kv, n_keys_b, e), BF16), pltpu.VMEM((n_kv, n_keys_b, e), BF16),
                        pltpu.VMEM((SUBLANES, n_keys_a), F32), pltpu.VMEM((SUBLANES, n_keys_a), F32),
                        pltpu.VMEM((SUBLANES, n_keys_b), F32), pltpu.VMEM((SUBLANES, n_keys_b), F32)],
        compiler_params=_params("arbitrary"),
        name="sample_attn",
    )(sinks, qa, ka_new, va_new, *([view(cache_ak)] * st), cache_ak, *([view(cache_av)] * st), cache_av,
      qb, kb_new, vb_new, cache_bk, cache_bv)


def _post_kernel(*refs, n_groups, tn):
    if n_groups:
        o_refs, refs = refs[:n_groups], refs[n_groups:]
        lse_refs, refs = refs[:n_groups], refs[n_groups:]
    else:
        ya_ref, refs = refs[0], refs[1:]
    yb_ref, ga_ref, gb_ref, x_ref, wba_ref, wbb_ref, wo_ref, out_ref, ya_s, mg_s = refs
    if n_groups:
        lses = [r[...] for r in lse_refs]
        os_ = [_native_to_heads(r[...]) for r in o_refs]
        for h in range(N_HEADS):
            e = [l[:, h:h + 1] for l in lses]
            m = functools.reduce(jnp.maximum, e)
            w = [jnp.exp(ei - m) for ei in e]
            num = sum(wi * o[h] for wi, o in zip(w, os_))
            ya_s[:, _head_cols(h)] = (num / sum(w)).astype(BF16)
    else:
        ya_s[...] = ya_ref[...].astype(BF16)
    d = out_ref.shape[1]
    yb = yb_ref[...].astype(BF16)
    for n in range(d // tn):
        sl = slice(n * tn, (n + 1) * tn)
        ya_d = jnp.dot(ya_s[...], wba_ref[:, sl], preferred_element_type=F32)
        yb_d = jnp.dot(yb, wbb_ref[:, sl], preferred_element_type=F32)
        mg_s[:, sl] = (ga_ref[:, sl].astype(F32) * ya_d + gb_ref[:, sl].astype(F32) * yb_d).astype(BF16)
    for n in range(d // tn):
        sl = slice(n * tn, (n + 1) * tn)
        out_ref[:, sl] = x_ref[:, sl] + jnp.dot(mg_s[...], wo_ref[:, sl], preferred_element_type=F32)


def _post(ya_parts, yb, gates, x, wba, wbb, wo, tm):
    m, d = x.shape
    wa = wba.shape[0]
    rows = lambda w, col=0: pl.BlockSpec((tm, w), lambda i: (i, col))
    whole = lambda a: pl.BlockSpec(a.shape, lambda i: (0, 0), pipeline_mode=pl.Buffered(1))
    if isinstance(ya_parts, tuple):
        os_, lses = ya_parts
        n_groups = len(os_)
        ya_args = list(os_) + list(lses)
        native = pl.BlockSpec((tm, N_HEADS, HEAD_DIM), lambda i: (i, 0, 0))
        ya_specs = [native] * n_groups + [rows(LANES)] * n_groups
    else:
        n_groups = 0
        ya_args = [ya_parts]
        ya_specs = [rows(wa)]
    return pl.pallas_call(
        functools.partial(_post_kernel, n_groups=n_groups, tn=512),
        grid=(m // tm,),
        in_specs=ya_specs + [rows(wa), rows(d, 0), rows(d, 1), rows(d), whole(wba), whole(wbb), whole(wo)],
        out_specs=rows(d),
        out_shape=jax.ShapeDtypeStruct((m, d), F32),
        scratch_shapes=[pltpu.VMEM((tm, wa), BF16), pltpu.VMEM((tm, d), BF16)],
        compiler_params=_params("parallel"),
        name="post_attn",
    )(*ya_args, yb, gates, gates, x, wba, wbb, wo)


def _ffn_kernel(x_ref, g_ref, wg_ref, wu_ref, wd_ref, out_ref, h_ref):
    f = pl.program_id(1)

    @pl.when(f == 0)
    def _():
        x = x_ref[...]
        h_ref[...] = _rms_rows(x, g_ref[...]).astype(BF16)
        out_ref[...] = x

    h = h_ref[...]
    gate = jnp.dot(h, wg_ref[...], preferred_element_type=F32)
    up = jnp.dot(h, wu_ref[...], preferred_element_type=F32)
    u = (jax.nn.silu(gate) * up).astype(BF16)
    out_ref[...] += jnp.dot(u, wd_ref[...], preferred_element_type=F32)


def _ffn(x, g, wg, wu, wd, tm, tf):
    m, d = x.shape
    dff = wg.shape[1]
    assert m % tm == 0 and dff % tf == 0
    return pl.pallas_call(
        _ffn_kernel,
        grid=(m // tm, dff // tf),
        in_specs=[pl.BlockSpec((tm, d), lambda i, f: (i, 0)), pl.BlockSpec((1, d), lambda i, f: (0, 0)),
                  pl.BlockSpec((d, tf), lambda i, f: (0, f)), pl.BlockSpec((d, tf), lambda i, f: (0, f)),
                  pl.BlockSpec((tf, d), lambda i, f: (f, 0))],
        out_specs=pl.BlockSpec((tm, d), lambda i, f: (i, 0)),
        out_shape=jax.ShapeDtypeStruct((m, d), F32),
        scratch_shapes=[pltpu.VMEM((tm, d), BF16)],
        compiler_params=_params("parallel", "arbitrary"),
        name="ffn",
    )(x, g, wg, wu, wd)


def _pad_rows(a, n):
    return jnp.pad(a, ((0, 0), (0, n - a.shape[1])) + ((0, 0),) * (a.ndim - 2))


def kernel(x_prompt, x_sample, cache_a_k, cache_a_v, cache_b_k, cache_b_v, norm_mix, w_in, qnorm_a, knorm_a,
           qnorm_b, knorm_b, sinks_b, w_branch_a, w_branch_b, w_out, norm_ffn, w_ffn_gate, w_ffn_up, w_ffn_down):
    depth = w_in.shape[0]
    b, t, d = x_prompt.shape
    sb, st, _ = x_sample.shape
    wa = N_HEADS * HEAD_DIM
    keep_b = min(BAND, t)
    xp = x_prompt.reshape(b * t, d)
    xs = x_sample.reshape(sb * st, d)
    tm_p = 512
    tm_s = sb * st
    row = lambda v: v.reshape(1, -1)
    outs = [[] for _ in range(8)]
    for l in range(depth):
        w_in_l = _pack_w_in(w_in[l], d)
        wba, wbb, wo = w_branch_a[l].astype(BF16), w_branch_b[l].astype(BF16), w_out[l].astype(BF16)
        wg, wu, wd = w_ffn_gate[l].astype(BF16), w_ffn_up[l].astype(BF16), w_ffn_down[l].astype(BF16)
        norms = (row(qnorm_a[l]), row(knorm_a[l]), row(qnorm_b[l]), row(knorm_b[l]))

        qa, ka, va, qb, kb, vb, gates = _inproj(xp, row(norm_mix[l]), w_in_l, *norms, tm=tm_p)
        seq = lambda a: a.reshape((b, t) + a.shape[1:])
        parts = [_band_attention_a(seq(qa), seq(ka), seq(va), dil) for dil in DILATIONS]
        os_ = [o.reshape(b * t, N_HEADS, HEAD_DIM) for o, _ in parts]
        lses = [s.reshape(b * t, LANES) for _, s in parts]
        yb = _band_attention_b(seq(qb), seq(kb), seq(vb), sinks_b[l]).reshape(b * t, wa)
        x1 = _post((os_, lses), yb, gates, xp, wba, wbb, wo, tm=256)
        xp = _ffn(x1, row(norm_ffn[l]), wg, wu, wd, tm=tm_p, tf=512)
        outs[0].append(seq(ka))
        outs[1].append(seq(va))
        outs[2].append(kb.reshape(b, t, N_KV_B, HEAD_DIM)[:, t - keep_b:])
        outs[3].append(vb.reshape(b, t, N_KV_B, HEAD_DIM)[:, t - keep_b:])

        qa, ka, va, qb, kb, vb, gates = _inproj(xs, row(norm_mix[l]), w_in_l, *norms, tm=tm_s)
        tok = lambda a: a.reshape((sb, st) + a.shape[1:])
        ya, yb = _sample_attention(
            l, st, sinks_b[l],
            _pad_rows(tok(qa), SUBLANES), _pad_rows(tok(ka), BF16_ROWS), _pad_rows(tok(va), BF16_ROWS),
            cache_a_k, cache_a_v,
            _pad_rows(tok(qb).astype(F32), SUBLANES), _pad_rows(tok(kb), BF16_ROWS), _pad_rows(tok(vb), BF16_ROWS),
            cache_b_k, cache_b_v)
        ya = ya[:, :st].reshape(sb * st, wa)
        yb = yb[:, :st].reshape(sb * st, wa)
        x1 = _post(ya, yb, gates, xs, wba, wbb, wo, tm=min(256, tm_s))
        xs = _ffn(x1, row(norm_ffn[l]), wg, wu, wd, tm=tm_s, tf=512)
        outs[4].append(tok(ka))
        outs[5].append(tok(va))
        outs[6].append(kb.reshape(sb, st, N_KV_B, HEAD_DIM))
        outs[7].append(vb.reshape(sb, st, N_KV_B, HEAD_DIM))
    return (xp.reshape(b, t, d), xs.reshape(sb, st, d)) + tuple(jnp.stack(o) for o in outs)
```

```python
import functools
import math

import jax
import jax.numpy as jnp
from jax import lax
from jax.experimental import pallas as pl
from jax.experimental.pallas import tpu as pltpu

F32 = jnp.float32
BF16 = jnp.bfloat16

HEAD_DIM = 128
N_HEADS = 8
N_KV_B = 2
BAND = 128
DILATIONS = (1, 4, 16)
MAX_DIL = DILATIONS[-1]
EPS = 1e-6
NEG = -1e30
LOG2E = math.log2(math.e)
Q_SCALE = HEAD_DIM ** -0.5 * LOG2E
LANES = 128
SUBLANES = 8
BF16_ROWS = 16
MXU_COLS = 256
VMEM_LIMIT_BYTES = 56 * 1024 * 1024

TN_IN = 1024


def _slope2(h):
    return 2.0 ** (-(h + 1)) * LOG2E


def _params(*sem):
    return pltpu.CompilerParams(dimension_semantics=sem, vmem_limit_bytes=VMEM_LIMIT_BYTES)


def _rms_rows(x, gain):
    ms = jnp.mean(x * x, axis=-1, keepdims=True)
    return x * lax.rsqrt(ms + EPS) * gain


def _head_cols(h):
    return slice(h * HEAD_DIM, (h + 1) * HEAD_DIM)


def _heads_to_native(parts):
    return jnp.swapaxes(jnp.stack(parts, axis=0), 0, 1)


def _native_to_heads(x):
    return jnp.swapaxes(x, 0, 1)


def _inproj_kernel(x_ref, g_ref, w_ref, qna_ref, kna_ref, qnb_ref, knb_ref,
                   qa_ref, ka_ref, va_ref, qb_ref, kb_ref, vb_ref, gates_ref, h_ref):
    j = pl.program_id(1)

    @pl.when(j == 0)
    def _():
        h_ref[...] = _rms_rows(x_ref[...], g_ref[...]).astype(BF16)

    def head_pairs(first=0, n=TN_IN // MXU_COLS):
        for c in range(first, first + n):
            p = jnp.dot(h_ref[...], w_ref[:, c * MXU_COLS:(c + 1) * MXU_COLS], preferred_element_type=F32)
            yield c, p[:, :HEAD_DIM], p[:, HEAD_DIM:]

    def normed_heads(gain):
        parts = []
        for _, lo, hi in head_pairs():
            parts += [_rms_rows(lo, gain), _rms_rows(hi, gain)]
        return parts

    @pl.when(j == 0)
    def _():
        qa_ref[...] = _heads_to_native(normed_heads(qna_ref[...] * Q_SCALE))

    @pl.when(j == 1)
    def _():
        ka_ref[...] = _heads_to_native(normed_heads(kna_ref[...]))

    @pl.when(j == 2)
    def _():
        parts = []
        for _, lo, hi in head_pairs():
            parts += [lo, hi]
        va_ref[...] = _heads_to_native(parts)

    @pl.when(j == 3)
    def _():
        gain = qnb_ref[...] * Q_SCALE
        for c, lo, hi in head_pairs():
            qb_ref[:, _head_cols(2 * c)] = _rms_rows(lo, gain).astype(BF16)
            qb_ref[:, _head_cols(2 * c + 1)] = _rms_rows(hi, gain).astype(BF16)

    @pl.when(j == 4)
    def _():
        for _, lo, hi in head_pairs(0, 1):
            kb_ref[:, _head_cols(0)] = _rms_rows(lo, knb_ref[...])
            kb_ref[:, _head_cols(1)] = _rms_rows(hi, knb_ref[...])
        for _, lo, hi in head_pairs(1, 1):
            vb_ref[:, _head_cols(0)] = lo
            vb_ref[:, _head_cols(1)] = hi

    @pl.when(j >= 5)
    def _():
        for c, lo, hi in head_pairs():
            gates_ref[:, _head_cols(2 * c)] = (0.5 * jnp.tanh(0.5 * lo) + 0.5).astype(BF16)
            gates_ref[:, _head_cols(2 * c + 1)] = (0.5 * jnp.tanh(0.5 * hi) + 0.5).astype(BF16)


def _pack_w_in(w, d):
    wa = N_HEADS * HEAD_DIM
    kvw = N_KV_B * HEAD_DIM
    assert w.shape[1] == 4 * wa + 2 * kvw + 2 * d and kvw == MXU_COLS and wa == TN_IN
    pad = jnp.zeros((w.shape[0], TN_IN - 2 * kvw), w.dtype)
    return jnp.concatenate([w[:, :4 * wa + 2 * kvw], pad, w[:, 4 * wa + 2 * kvw:]], axis=1).astype(BF16)


def _inproj(x, g, w, qna, kna, qnb, knb, tm):
    m, d = x.shape
    wa = N_HEADS * HEAD_DIM
    kvw = N_KV_B * HEAD_DIM
    assert m % tm == 0 and d % TN_IN == 0
    ng = 2 * d // TN_IN
    nj = 5 + ng
    assert w.shape == (d, nj * TN_IN)
    native = pl.BlockSpec((tm, N_HEADS, HEAD_DIM), lambda i, j: (i, 0, 0))
    flat = lambda width: pl.BlockSpec((tm, width), lambda i, j: (i, 0))
    row = lambda n: pl.BlockSpec((1, n), lambda i, j: (0, 0))
    nat_shape = jax.ShapeDtypeStruct((m, N_HEADS, HEAD_DIM), F32)
    return pl.pallas_call(
        _inproj_kernel,
        grid=(m // tm, nj),
        in_specs=[pl.BlockSpec((tm, d), lambda i, j: (i, 0)), row(d),
                  pl.BlockSpec((d, TN_IN), lambda i, j: (0, j)),
                  row(HEAD_DIM), row(HEAD_DIM), row(HEAD_DIM), row(HEAD_DIM)],
        out_specs=[native, native, native, flat(wa), flat(kvw), flat(kvw),
                   pl.BlockSpec((tm, TN_IN), lambda i, j: (i, jnp.clip(j - 5, 0, ng - 1)))],
        out_shape=[nat_shape, nat_shape, nat_shape, jax.ShapeDtypeStruct((m, wa), BF16),
                   jax.ShapeDtypeStruct((m, kvw), F32), jax.ShapeDtypeStruct((m, kvw), F32),
                   jax.ShapeDtypeStruct((m, 2 * d), BF16)],
        scratch_shapes=[pltpu.VMEM((tm, d), BF16)],
        compiler_params=_params("parallel", "arbitrary"),
        name="inproj",
    )(x, g, w, qna, kna, qnb, knb)


def _band_kernel(*refs, n_kv, lc, dil, sink, native):
    if sink:
        sink_ref, q_ref, kh_ref, k_ref, vh_ref, v_ref, o_ref, qbuf, kbuf, vbuf, bias_s = refs
    else:
        q_ref, kh_ref, k_ref, vh_ref, v_ref, o_ref, lse_ref, qbuf, kbuf, vbuf, bias_s = refs
    c = pl.program_id(2)

    @pl.when((pl.program_id(0) == 0) & (pl.program_id(1) == 0) & (c == 0))
    def _():
        rows = lax.broadcasted_iota(jnp.int32, (BAND, 2 * BAND), 0)
        cols = lax.broadcasted_iota(jnp.int32, (BAND, 2 * BAND), 1)
        back = rows + BAND - cols
        in_band = (back >= 0) & (back <= BAND)
        backf = back.astype(F32)
        for h in range(N_HEADS):
            bias = -(_slope2(h) * dil) * backf
            bias_s[0, h * BAND:(h + 1) * BAND, :] = jnp.where(in_band, bias, NEG)
            bias_s[1, h * BAND:(h + 1) * BAND, :] = jnp.where(in_band & (cols >= BAND), bias, NEG)

    if native:
        heads = lambda ref: _native_to_heads(ref[...]).astype(BF16)
        qbuf[...] = heads(q_ref)
        for buf, halo_ref, ref in ((kbuf, kh_ref, k_ref), (vbuf, vh_ref, v_ref)):
            buf[:, 0:BAND, :] = heads(halo_ref)
            buf[:, BAND:, :] = heads(ref)
    else:
        for h in range(N_HEADS):
            qbuf[h] = q_ref[:, _head_cols(h)]
        for buf, halo_ref, ref in ((kbuf, kh_ref, k_ref), (vbuf, vh_ref, v_ref)):
            for h in range(n_kv):
                buf[h, 0:BAND, :] = halo_ref[:, _head_cols(h)].astype(BF16)
                buf[h, BAND:, :] = ref[:, _head_cols(h)].astype(BF16)

    lane = lax.broadcasted_iota(jnp.int32, (BAND, LANES), 1)
    group_shift = (N_HEADS // n_kv).bit_length() - 1
    assert 1 << group_shift == N_HEADS // n_kv

    def block(i, carry):
        r0 = pl.multiple_of(i * BAND, BAND)
        variant = jnp.where((c == 0) & (i == 0), 1, 0)
        nt = (((1,), (1,)), ((), ()))
        s = jnp.concatenate(
            [lax.dot_general(qbuf[h, pl.ds(r0, BAND), :], kbuf[h >> group_shift, pl.ds(r0, 2 * BAND), :], nt,
                             preferred_element_type=F32) for h in range(N_HEADS)], axis=0)
        s = s + bias_s[variant]
        m = jnp.max(s, axis=-1, keepdims=True)
        p = jnp.exp2(s - m)
        l = jnp.sum(p, axis=-1, keepdims=True)
        lse_tile = jnp.zeros((BAND, LANES), F32)
        ys = []
        for h in range(N_HEADS):
            rs = slice(h * BAND, (h + 1) * BAND)
            o = jnp.dot(p[rs].astype(BF16), vbuf[h >> group_shift, pl.ds(r0, 2 * BAND), :],
                        preferred_element_type=F32)
            mh, lh = m[rs], l[rs]
            if sink:
                sk = sink_ref[h] * LOG2E
                mm = jnp.maximum(mh, sk)
                a = jnp.exp2(mh - mm)
                ys.append(o * a / (lh * a + jnp.exp2(sk - mm)))
            else:
                ys.append(o / lh)
                lse_tile = jnp.where(lane == h, mh + jnp.log2(lh), lse_tile)
        if native:
            o_ref[pl.ds(r0, BAND)] = _heads_to_native(ys)
        else:
            for h in range(N_HEADS):
                o_ref[pl.ds(r0, BAND), _head_cols(h)] = ys[h].astype(BF16)
        if not sink:
            lse_ref[pl.ds(r0, BAND), :] = lse_tile
        return carry

    lax.fori_loop(0, lc // BAND, block, 0)


def _band_chunks(t, dil):
    tl = t // dil
    lc = min(tl, 4 * BAND)
    assert t % dil == 0 and tl % lc == 0 and lc % BAND == 0
    return tl, lc


def _band_scratch(n_kv, lc):
    return [pltpu.VMEM((N_HEADS, lc, HEAD_DIM), BF16),
            pltpu.VMEM((n_kv, lc + BAND, HEAD_DIM), BF16), pltpu.VMEM((n_kv, lc + BAND, HEAD_DIM), BF16),
            pltpu.VMEM((2, N_HEADS * BAND, 2 * BAND), F32)]


def _band_attention_a(q, k, v, dil):
    b, t, nh, e = q.shape
    tl, lc = _band_chunks(t, dil)
    view = lambda a: a.reshape(b, tl, dil, nh, e)
    chunk = pl.BlockSpec((None, lc, None, nh, e), lambda bi, r, c: (bi, c, r, 0, 0))
    halo = pl.BlockSpec((None, BAND, None, nh, e),
                        lambda bi, r, c: (bi, jnp.maximum(c * (lc // BAND) - 1, 0), r, 0, 0))
    o, lse = pl.pallas_call(
        functools.partial(_band_kernel, n_kv=nh, lc=lc, dil=dil, sink=False, native=True),
        grid=(b, dil, tl // lc),
        in_specs=[chunk, halo, chunk, halo, chunk],
        out_specs=[chunk, pl.BlockSpec((None, lc, LANES), lambda bi, r, c: (bi, c, r))],
        out_shape=[jax.ShapeDtypeStruct((b, tl, dil, nh, e), F32),
                   jax.ShapeDtypeStruct((b, tl, dil * LANES), F32)],
        scratch_shapes=_band_scratch(nh, lc),
        compiler_params=_params("arbitrary", "arbitrary", "arbitrary"),
        name=f"band_d{dil}",
    )(view(q), view(k), view(k), view(v), view(v))
    return o.reshape(b, t, nh, e), lse.reshape(b, t, LANES)


def _band_attention_b(q, k, v, sinks):
    b, t, wq = q.shape
    wk = k.shape[2]
    tl, lc = _band_chunks(t, 1)
    chunk = lambda w: pl.BlockSpec((None, lc, w), lambda bi, r, c: (bi, c, 0))
    halo = pl.BlockSpec((None, BAND, wk), lambda bi, r, c: (bi, jnp.maximum(c * (lc // BAND) - 1, 0), 0))
    n_kv = wk // HEAD_DIM
    return pl.pallas_call(
        functools.partial(_band_kernel, n_kv=n_kv, lc=lc, dil=1, sink=True, native=False),
        grid=(b, 1, tl // lc),
        in_specs=[pl.BlockSpec(memory_space=pltpu.SMEM), chunk(wq), halo, chunk(wk), halo, chunk(wk)],
        out_specs=chunk(wq),
        out_shape=jax.ShapeDtypeStruct((b, t, wq), BF16),
        scratch_shapes=_band_scratch(n_kv, lc),
        compiler_params=_params("arbitrary", "arbitrary", "arbitrary"),
        name="band_sink",
    )(sinks, q, k, k, v, v)


def _sample_key_table(past, st, n_far, n_near, n_keys, dilations):
    shape = (N_HEADS * SUBLANES, n_keys)
    rows = lax.broadcasted_iota(jnp.int32, shape, 0)
    j = lax.broadcasted_iota(jnp.int32, shape, 1)
    per_res = max(n_far // st, 1)
    far = (j % per_res) * MAX_DIL + j // per_res
    near = past - n_near + (j - n_far)
    n = jnp.where(j < n_far, far, near)
    dist = past + (rows & (SUBLANES - 1)) - n
    cnt = jnp.zeros(shape, F32)
    for dil in dilations:
        ok = (dist >= 0) & (dist <= BAND * dil) & ((dist & (dil - 1)) == 0)
        cnt = cnt + ok.astype(F32)
    slope2 = jnp.exp2(-((rows >> 3) + 1).astype(F32)) * LOG2E
    return cnt, jnp.where(cnt > 0.0, -slope2 * dist.astype(F32), NEG)


def _sample_softmax(q, kall, vall, cnt, bias):
    nt = (((1,), (1,)), ((), ()))
    group = N_HEADS // kall.shape[0]
    s = jnp.concatenate([lax.dot_general(q[h], kall[h // group], nt, preferred_element_type=F32)
                         for h in range(N_HEADS)], axis=0)
    s = s + bias
    m = jnp.max(s, axis=-1, keepdims=True)
    p = cnt * jnp.exp2(s - m)
    l = jnp.sum(p, axis=-1, keepdims=True)
    o = jnp.concatenate([jnp.dot(p[h * SUBLANES:(h + 1) * SUBLANES].astype(BF16), vall[h // group],
                                 preferred_element_type=F32) for h in range(N_HEADS)], axis=0)
    return m, l, o


def _sample_kernel(*refs, past_a, past_b, st):
    sink_ref, qa_ref, kan_ref, van_ref = refs[:4]
    kfar_refs, knear_ref = refs[4:4 + st], refs[4 + st]
    vfar_refs, vnear_ref = refs[5 + st:5 + 2 * st], refs[5 + 2 * st]
    (qb_ref, kbn_ref, vbn_ref, cbk_ref, cbv_ref, ya_ref, yb_ref,
     kall_a, vall_a, kall_b, vall_b, cnt_a, bias_a, cnt_b, bias_b) = refs[6 + 2 * st:]
    per_res = kfar_refs[0].shape[0]
    n_far = st * per_res
    n_near = knear_ref.shape[0]
    n_new = kan_ref.shape[0]
    used_a = n_far + n_near + n_new
    used_b = past_b + n_new

    @pl.when(pl.program_id(0) == 0)
    def _():
        for buf, used in ((kall_a, used_a), (vall_a, used_a), (kall_b, used_b), (vall_b, used_b)):
            buf[:, used:, :] = jnp.zeros((buf.shape[0], buf.shape[1] - used, buf.shape[2]), BF16)
        ca, ba = _sample_key_table(past_a, st, n_far, n_near, kall_a.shape[1], DILATIONS)
        cnt_a[...] = ca
        bias_a[...] = ba
        cb, bb = _sample_key_table(past_b, st, 0, past_b, kall_b.shape[1], (1,))
        cnt_b[...] = cb
        bias_b[...] = bb

    heads = lambda ref: _native_to_heads(ref[...]).astype(BF16)
    for far_refs, src_near, src_new, dst in ((kfar_refs, knear_ref, kan_ref, kall_a),
                                             (vfar_refs, vnear_ref, van_ref, vall_a)):
        for r, src_far in enumerate(far_refs):
            dst[:, r * per_res:(r + 1) * per_res, :] = heads(src_far)
        dst[:, n_far:n_far + n_near, :] = heads(src_near)
        dst[:, n_far + n_near:used_a, :] = heads(src_new)
    for src_old, src_new, dst in ((cbk_ref, kbn_ref, kall_b), (cbv_ref, vbn_ref, vall_b)):
        for h in range(N_KV_B):
            dst[h, 0:past_b, :] = src_old[:, h, :].astype(BF16)
            dst[h, past_b:used_b, :] = src_new[:, _head_cols(h)].astype(BF16)

    _, l, o = _sample_softmax(heads(qa_ref), kall_a, vall_a, cnt_a[...], bias_a[...])
    y = o / l
    for h in range(N_HEADS):
        ya_ref[:, _head_cols(h)] = y[h * SUBLANES:(h + 1) * SUBLANES]

    qb = jnp.stack([qb_ref[:, _head_cols(h)] for h in range(N_HEADS)], axis=0).astype(BF16)
    m, l, o = _sample_softmax(qb, kall_b, vall_b, cnt_b[...], bias_b[...])
    rows = lax.broadcasted_iota(jnp.int32, m.shape, 0)
    sk = jnp.zeros(m.shape, F32)
    for h in range(N_HEADS):
        sk = jnp.where(rows >> 3 == h, sink_ref[h] * LOG2E, sk)
    mm = jnp.maximum(m, sk)
    a = jnp.exp2(m - mm)
    y = o * a / (l * a + jnp.exp2(sk - mm))
    for h in range(N_HEADS):
        yb_ref[:, _head_cols(h)] = y[h * SUBLANES:(h + 1) * SUBLANES]


def _sample_attention(layer, st, sinks, qa, ka_new, va_new, cache_ak, cache_av, qb, kb_new, vb_new,
                      cache_bk, cache_bv):
    depth, b, past_a, nh, e = cache_ak.shape
    past_b, n_kv = cache_bk.shape[2], cache_bk.shape[3]
    n_new = ka_new.shape[1]
    wq = nh * e
    near_rows = BAND * DILATIONS[-2]
    assert past_a == BAND * MAX_DIL and st <= DILATIONS[-2] and past_b >= BAND and st <= SUBLANES
    assert past_b % BF16_ROWS == 0 and n_new % BF16_ROWS == 0
    n_far_blocks = (past_a - near_rows) // MAX_DIL
    n_keys_a = pl.cdiv(n_far_blocks * st + near_rows + n_new, LANES) * LANES
    n_keys_b = pl.cdiv(past_b + n_new, LANES) * LANES
    assert n_far_blocks % BF16_ROWS == 0 and near_rows % BF16_ROWS == 0 and past_a % near_rows == 0
    view = lambda c: c.reshape(depth, b, past_a // MAX_DIL, MAX_DIL, nh, e)
    far = [pl.BlockSpec((None, None, n_far_blocks, None, nh, e), lambda i, r=r: (layer, i, 0, r, 0, 0))
           for r in range(st)]
    near = pl.BlockSpec((None, None, near_rows, nh, e), lambda i: (layer, i, past_a // near_rows - 1, 0, 0))
    per_b = lambda *shape: pl.BlockSpec((None,) + shape, lambda i: (i,) + (0,) * len(shape))
    cache_b = pl.BlockSpec((None, None, past_b, n_kv, e), lambda i: (layer, i, 0, 0, 0))
    table = lambda n: pltpu.VMEM((N_HEADS * SUBLANES, n), F32)
    return pl.pallas_call(
        functools.partial(_sample_kernel, past_a=past_a, past_b=past_b, st=st),
        grid=(b,),
        in_specs=[pl.BlockSpec(memory_space=pltpu.SMEM),
                  per_b(SUBLANES, nh, e), per_b(n_new, nh, e), per_b(n_new, nh, e), *far, near, *far, near,
                  per_b(SUBLANES, wq), per_b(n_new, n_kv * e), per_b(n_new, n_kv * e), cache_b, cache_b],
        out_specs=[per_b(SUBLANES, wq), per_b(SUBLANES, wq)],
        out_shape=[jax.ShapeDtypeStruct((b, SUBLANES, wq), F32), jax.ShapeDtypeStruct((b, SUBLANES, wq), F32)],
        scratch_shapes=[pltpu.VMEM((nh, n_keys_a, e), BF16), pltpu.VMEM((nh, n_keys_a, e), BF16),
                        pltpu.VMEM((n_kv, n_keys_b, e), BF16), pltpu.VMEM((n_kv, n_keys_b, e), BF16),
                        table(n_keys_a), table(n_keys_a), table(n_keys_b), table(n_keys_b)],
        compiler_params=_params("arbitrary"),
        name="sample_attn",
    )(sinks, qa, ka_new, va_new, *([view(cache_ak)] * st), cache_ak, *([view(cache_av)] * st), cache_av,
      qb, kb_new, vb_new, cache_bk, cache_bv)


def _post_kernel(*refs, n_groups, tn):
    if n_groups:
        o_refs, refs = refs[:n_groups], refs[n_groups:]
        lse_refs, refs = refs[:n_groups], refs[n_groups:]
    else:
        ya_ref, refs = refs[0], refs[1:]
    yb_ref, ga_ref, gb_ref, x_ref, wba_ref, wbb_ref, wo_ref, out_ref, ya_s, mg_s = refs
    if n_groups:
        lses = [r[...] for r in lse_refs]
        os_ = [_native_to_heads(r[...]) for r in o_refs]
        for h in range(N_HEADS):
            e = [l[:, h:h + 1] for l in lses]
            m = functools.reduce(jnp.maximum, e)
            w = [jnp.exp2(ei - m) for ei in e]
            num = sum(wi * o[h] for wi, o in zip(w, os_))
            ya_s[:, _head_cols(h)] = (num / sum(w)).astype(BF16)
    else:
        ya_s[...] = ya_ref[...].astype(BF16)
    d = out_ref.shape[1]
    yb = yb_ref[...].astype(BF16)
    for n in range(d // tn):
        sl = slice(n * tn, (n + 1) * tn)
        ya_d = jnp.dot(ya_s[...], wba_ref[:, sl], preferred_element_type=F32)
        yb_d = jnp.dot(yb, wbb_ref[:, sl], preferred_element_type=F32)
        mg_s[:, sl] = (ga_ref[:, sl].astype(F32) * ya_d + gb_ref[:, sl].astype(F32) * yb_d).astype(BF16)
    for n in range(d // tn):
        sl = slice(n * tn, (n + 1) * tn)
        out_ref[:, sl] = x_ref[:, sl] + jnp.dot(mg_s[...], wo_ref[:, sl], preferred_element_type=F32)


def _post(ya_parts, yb, gates, x, wba, wbb, wo, tm):
    m, d = x.shape
    wa = wba.shape[0]
    rows = lambda w, col=0: pl.BlockSpec((tm, w), lambda i: (i, col))
    whole = lambda a: pl.BlockSpec(a.shape, lambda i: (0, 0), pipeline_mode=pl.Buffered(1))
    if isinstance(ya_parts, tuple):
        os_, lses = ya_parts
        n_groups = len(os_)
        ya_args = list(os_) + list(lses)
        native = pl.BlockSpec((tm, N_HEADS, HEAD_DIM), lambda i: (i, 0, 0))
        ya_specs = [native] * n_groups + [rows(LANES)] * n_groups
    else:
        n_groups = 0
        ya_args = [ya_parts]
        ya_specs = [rows(wa)]
    return pl.pallas_call(
        functools.partial(_post_kernel, n_groups=n_groups, tn=512),
        grid=(m // tm,),
        in_specs=ya_specs + [rows(wa), rows(d, 0), rows(d, 1), rows(d), whole(wba), whole(wbb), whole(wo)],
        out_specs=rows(d),
        out_shape=jax.ShapeDtypeStruct((m, d), F32),
        scratch_shapes=[pltpu.VMEM((tm, wa), BF16), pltpu.VMEM((tm, d), BF16)],
        compiler_params=_params("parallel"),
        name="post_attn",
    )(*ya_args, yb, gates, gates, x, wba, wbb, wo)


def _ffn_kernel(x_ref, g_ref, wg_ref, wu_ref, wd_ref, out_ref, h_ref):
    f = pl.program_id(1)

    @pl.when(f == 0)
    def _():
        x = x_ref[...]
        h_ref[...] = _rms_rows(x, g_ref[...]).astype(BF16)
        out_ref[...] = x

    h = h_ref[...]
    gate = jnp.dot(h, wg_ref[...], preferred_element_type=F32)
    up = jnp.dot(h, wu_ref[...], preferred_element_type=F32)
    u = (jax.nn.silu(gate) * up).astype(BF16)
    out_ref[...] += jnp.dot(u, wd_ref[...], preferred_element_type=F32)


def _ffn(x, g, wg, wu, wd, tm, tf):
    m, d = x.shape
    dff = wg.shape[1]
    assert m % tm == 0 and dff % tf == 0
    return pl.pallas_call(
        _ffn_kernel,
        grid=(m // tm, dff // tf),
        in_specs=[pl.BlockSpec((tm, d), lambda i, f: (i, 0)), pl.BlockSpec((1, d), lambda i, f: (0, 0)),
                  pl.BlockSpec((d, tf), lambda i, f: (0, f)), pl.BlockSpec((d, tf), lambda i, f: (0, f)),
                  pl.BlockSpec((tf, d), lambda i, f: (f, 0))],
        out_specs=pl.BlockSpec((tm, d), lambda i, f: (i, 0)),
        out_shape=jax.ShapeDtypeStruct((m, d), F32),
        scratch_shapes=[pltpu.VMEM((tm, d), BF16)],
        compiler_params=_params("parallel", "arbitrary"),
        name="ffn",
    )(x, g, wg, wu, wd)


def _pad_rows(a, n):
    return jnp.pad(a, ((0, 0), (0, n - a.shape[1])) + ((0, 0),) * (a.ndim - 2))


def kernel(x_prompt, x_sample, cache_a_k, cache_a_v, cache_b_k, cache_b_v, norm_mix, w_in, qnorm_a, knorm_a,
           qnorm_b, knorm_b, sinks_b, w_branch_a, w_branch_b, w_out, norm_ffn, w_ffn_gate, w_ffn_up, w_ffn_down):
    depth = w_in.shape[0]
    b, t, d = x_prompt.shape
    sb, st, _ = x_sample.shape
    wa = N_HEADS * HEAD_DIM
    keep_b = min(BAND, t)
    xp = x_prompt.reshape(b * t, d)
    xs = x_sample.reshape(sb * st, d)
    tm_p = 512
    tm_s = sb * st
    row = lambda v: v.reshape(1, -1)
    outs = [[] for _ in range(8)]
    for l in range(depth):
        w_in_l = _pack_w_in(w_in[l], d)
        wba, wbb, wo = w_branch_a[l].astype(BF16), w_branch_b[l].astype(BF16), w_out[l].astype(BF16)
        wg, wu, wd = w_ffn_gate[l].astype(BF16), w_ffn_up[l].astype(BF16), w_ffn_down[l].astype(BF16)
        norms = (row(qnorm_a[l]), row(knorm_a[l]), row(qnorm_b[l]), row(knorm_b[l]))

        qa, ka, va, qb, kb, vb, gates = _inproj(xp, row(norm_mix[l]), w_in_l, *norms, tm=tm_p)
        seq = lambda a: a.reshape((b, t) + a.shape[1:])
        parts = [_band_attention_a(seq(qa), seq(ka), seq(va), dil) for dil in DILATIONS]
        os_ = [o.reshape(b * t, N_HEADS, HEAD_DIM) for o, _ in parts]
        lses = [s.reshape(b * t, LANES) for _, s in parts]
        yb = _band_attention_b(seq(qb), seq(kb), seq(vb), sinks_b[l]).reshape(b * t, wa)
        x1 = _post((os_, lses), yb, gates, xp, wba, wbb, wo, tm=256)
        xp = _ffn(x1, row(norm_ffn[l]), wg, wu, wd, tm=tm_p, tf=512)
        outs[0].append(seq(ka))
        outs[1].append(seq(va))
        outs[2].append(kb.reshape(b, t, N_KV_B, HEAD_DIM)[:, t - keep_b:])
        outs[3].append(vb.reshape(b, t, N_KV_B, HEAD_DIM)[:, t - keep_b:])

        qa, ka, va, qb, kb, vb, gates = _inproj(xs, row(norm_mix[l]), w_in_l, *norms, tm=tm_s)
        tok = lambda a: a.reshape((sb, st) + a.shape[1:])
        ya, yb = _sample_attention(
            l, st, sinks_b[l],
            _pad_rows(tok(qa), SUBLANES), _pad_rows(tok(ka), BF16_ROWS), _pad_rows(tok(va), BF16_ROWS),
            cache_a_k, cache_a_v,
            _pad_rows(tok(qb).astype(F32), SUBLANES), _pad_rows(tok(kb), BF16_ROWS), _pad_rows(tok(vb), BF16_ROWS),
            cache_b_k, cache_b_v)
        ya = ya[:, :st].reshape(sb * st, wa)
        yb = yb[:, :st].reshape(sb * st, wa)
        x1 = _post(ya, yb, gates, xs, wba, wbb, wo, tm=min(256, tm_s))
        xs = _ffn(x1, row(norm_ffn[l]), wg, wu, wd, tm=tm_s, tf=512)
        outs[4].append(tok(ka))
        outs[5].append(tok(va))
        outs[6].append(kb.reshape(sb, st, N_KV_B, HEAD_DIM))
        outs[7].append(vb.reshape(sb, st, N_KV_B, HEAD_DIM))
    return (xp.reshape(b, t, d), xs.reshape(sb, st, d)) + tuple(jnp.stack(o) for o in outs)
```

```python
import functools
import math

import jax
import jax.numpy as jnp
from jax import lax
from jax.experimental import pallas as pl
from jax.experimental.pallas import tpu as pltpu

F32 = jnp.float32
BF16 = jnp.bfloat16

HEAD_DIM = 128
N_HEADS = 8
N_KV_B = 2
BAND = 128
DILATIONS = (1, 4, 16)
MAX_DIL = DILATIONS[-1]
EPS = 1e-6
NEG = -1e30
LOG2E = math.log2(math.e)
Q_SCALE = HEAD_DIM ** -0.5 * LOG2E
LANES = 128
SUBLANES = 8
BF16_ROWS = 16
MXU_COLS = 256
VMEM_LIMIT_BYTES = 56 * 1024 * 1024

TN_IN = 1024


def _slope2(h):
    return 2.0 ** (-(h + 1)) * LOG2E


def _params(*sem):
    return pltpu.CompilerParams(dimension_semantics=sem, vmem_limit_bytes=VMEM_LIMIT_BYTES)


def _rms_rows(x, gain):
    ms = jnp.mean(x * x, axis=-1, keepdims=True)
    return x * lax.rsqrt(ms + EPS) * gain


def _head_cols(h):
    return slice(h * HEAD_DIM, (h + 1) * HEAD_DIM)


def _heads_to_native(parts):
    return jnp.swapaxes(jnp.stack(parts, axis=0), 0, 1)


def _native_to_heads(x):
    return jnp.swapaxes(x, 0, 1)


def _inproj_kernel(*refs, n_prev):
    x_ref, g_ref, w_ref, qna_ref, kna_ref, qnb_ref, knb_ref = refs[:7]
    kprev_ref, vprev_ref = refs[7:9] if n_prev else (None, None)
    qa_ref, ka_ref, va_ref, qb_ref, kb_ref, vb_ref, gates_ref, h_ref = refs[9 if n_prev else 7:]
    j = pl.program_id(1)

    @pl.when(j == 0)
    def _():
        h_ref[...] = _rms_rows(x_ref[...], g_ref[...]).astype(BF16)

    def head_pairs(first=0, n=TN_IN // MXU_COLS):
        for c in range(first, first + n):
            p = jnp.dot(h_ref[...], w_ref[:, c * MXU_COLS:(c + 1) * MXU_COLS], preferred_element_type=F32)
            yield c, p[:, :HEAD_DIM], p[:, HEAD_DIM:]

    def normed_heads(gain):
        parts = []
        for _, lo, hi in head_pairs():
            parts += [_rms_rows(lo, gain), _rms_rows(hi, gain)]
        return parts

    @pl.when(j == 0)
    def _():
        qa_ref[...] = _heads_to_native(normed_heads(qna_ref[...] * Q_SCALE))

    @pl.when(j == 1)
    def _():
        ka_ref[n_prev] = _heads_to_native(normed_heads(kna_ref[...]))
        if n_prev:
            ka_ref[0:n_prev] = kprev_ref[...]

    @pl.when(j == 2)
    def _():
        parts = []
        for _, lo, hi in head_pairs():
            parts += [lo, hi]
        va_ref[n_prev] = _heads_to_native(parts)
        if n_prev:
            va_ref[0:n_prev] = vprev_ref[...]

    @pl.when(j == 3)
    def _():
        gain = qnb_ref[...] * Q_SCALE
        for c, lo, hi in head_pairs():
            qb_ref[:, _head_cols(2 * c)] = _rms_rows(lo, gain).astype(BF16)
            qb_ref[:, _head_cols(2 * c + 1)] = _rms_rows(hi, gain).astype(BF16)

    @pl.when(j == 4)
    def _():
        for _, lo, hi in head_pairs(0, 1):
            kb_ref[:, _head_cols(0)] = _rms_rows(lo, knb_ref[...])
            kb_ref[:, _head_cols(1)] = _rms_rows(hi, knb_ref[...])
        for _, lo, hi in head_pairs(1, 1):
            vb_ref[:, _head_cols(0)] = lo
            vb_ref[:, _head_cols(1)] = hi

    @pl.when(j >= 5)
    def _():
        for c, lo, hi in head_pairs():
            gates_ref[:, _head_cols(2 * c)] = (0.5 * jnp.tanh(0.5 * lo) + 0.5).astype(BF16)
            gates_ref[:, _head_cols(2 * c + 1)] = (0.5 * jnp.tanh(0.5 * hi) + 0.5).astype(BF16)


def _pack_w_in(w, d):
    wa = N_HEADS * HEAD_DIM
    kvw = N_KV_B * HEAD_DIM
    assert w.shape[1] == 4 * wa + 2 * kvw + 2 * d and kvw == MXU_COLS and wa == TN_IN
    pad = jnp.zeros((w.shape[0], TN_IN - 2 * kvw), w.dtype)
    return jnp.concatenate([w[:, :4 * wa + 2 * kvw], pad, w[:, 4 * wa + 2 * kvw:]], axis=1).astype(BF16)


def _inproj(x, g, w, qna, kna, qnb, knb, tm, prev=None):
    m, d = x.shape
    wa = N_HEADS * HEAD_DIM
    kvw = N_KV_B * HEAD_DIM
    assert m % tm == 0 and d % TN_IN == 0
    ng = 2 * d // TN_IN
    nj = 5 + ng
    assert w.shape == (d, nj * TN_IN)
    native = pl.BlockSpec((tm, N_HEADS, HEAD_DIM), lambda i, j: (i, 0, 0))
    flat = lambda width: pl.BlockSpec((tm, width), lambda i, j: (i, 0))
    row = lambda n: pl.BlockSpec((1, n), lambda i, j: (0, 0))
    n_prev = 0 if prev is None else prev[0].shape[0]
    slots = lambda n: pl.BlockSpec((n, tm, N_HEADS, HEAD_DIM), lambda i, j: (0, i, 0, 0))
    slots_shape = jax.ShapeDtypeStruct((n_prev + 1, m, N_HEADS, HEAD_DIM), F32)
    return pl.pallas_call(
        functools.partial(_inproj_kernel, n_prev=n_prev),
        grid=(m // tm, nj),
        in_specs=[pl.BlockSpec((tm, d), lambda i, j: (i, 0)), row(d),
                  pl.BlockSpec((d, TN_IN), lambda i, j: (0, j)),
                  row(HEAD_DIM), row(HEAD_DIM), row(HEAD_DIM), row(HEAD_DIM)] + [slots(n_prev)] * (2 if n_prev else 0),
        out_specs=[native, slots(n_prev + 1), slots(n_prev + 1), flat(wa), flat(kvw), flat(kvw),
                   pl.BlockSpec((tm, TN_IN), lambda i, j: (i, jnp.clip(j - 5, 0, ng - 1)))],
        out_shape=[jax.ShapeDtypeStruct((m, N_HEADS, HEAD_DIM), F32), slots_shape, slots_shape,
                   jax.ShapeDtypeStruct((m, wa), BF16),
                   jax.ShapeDtypeStruct((m, kvw), F32), jax.ShapeDtypeStruct((m, kvw), F32),
                   jax.ShapeDtypeStruct((m, 2 * d), BF16)],
        scratch_shapes=[pltpu.VMEM((tm, d), BF16)],
        compiler_params=_params("parallel", "arbitrary"),
        name="inproj",
    )(x, g, w, qna, kna, qnb, knb, *(prev or ()))


def _band_kernel(*refs, n_kv, lc, dil, sink, native, halo):
    refs = list(refs)
    sink_ref = refs.pop(0) if sink else None
    q_ref = refs.pop(0)
    kh_ref = refs.pop(0) if halo else None
    k_ref = refs.pop(0)
    vh_ref = refs.pop(0) if halo else None
    v_ref, o_ref = refs.pop(0), refs.pop(0)
    lse_ref = None if sink else refs.pop(0)
    qbuf, kbuf, vbuf, bias_s = refs
    c = pl.program_id(2)
    off = kbuf.shape[1] - lc
    kw = BAND + off

    @pl.when((pl.program_id(0) == 0) & (pl.program_id(1) == 0) & (c == 0))
    def _():
        rows = lax.broadcasted_iota(jnp.int32, (BAND, kw), 0)
        cols = lax.broadcasted_iota(jnp.int32, (BAND, kw), 1)
        back = rows + off - cols
        in_band = (back >= 0) & (back <= BAND)
        backf = back.astype(F32)
        for h in range(N_HEADS):
            bias = -(_slope2(h) * dil) * backf
            bias_s[0, h * BAND:(h + 1) * BAND, :] = jnp.where(in_band, bias, NEG)
            bias_s[1, h * BAND:(h + 1) * BAND, :] = jnp.where(in_band & (cols >= off), bias, NEG)
        if off and not halo:
            for buf in (kbuf, vbuf):
                buf[:, 0:off, :] = jnp.zeros((buf.shape[0], off, buf.shape[2]), BF16)

    if native:
        heads = lambda ref: _native_to_heads(ref[...]).astype(BF16)
        qbuf[...] = heads(q_ref)
        for buf, halo_ref, ref in ((kbuf, kh_ref, k_ref), (vbuf, vh_ref, v_ref)):
            if halo:
                buf[:, 0:off, :] = heads(halo_ref)
            buf[:, off:, :] = heads(ref)
    else:
        for h in range(N_HEADS):
            qbuf[h] = q_ref[:, _head_cols(h)]
        for buf, halo_ref, ref in ((kbuf, kh_ref, k_ref), (vbuf, vh_ref, v_ref)):
            for h in range(n_kv):
                if halo:
                    buf[h, 0:off, :] = halo_ref[:, _head_cols(h)].astype(BF16)
                buf[h, off:, :] = ref[:, _head_cols(h)].astype(BF16)

    lane = lax.broadcasted_iota(jnp.int32, (BAND, LANES), 1)
    group_shift = (N_HEADS // n_kv).bit_length() - 1
    assert 1 << group_shift == N_HEADS // n_kv

    def block(i, carry):
        r0 = pl.multiple_of(i * BAND, BAND)
        variant = jnp.where((c == 0) & (i == 0), 1, 0)
        nt = (((1,), (1,)), ((), ()))
        s = jnp.concatenate(
            [lax.dot_general(qbuf[h, pl.ds(r0, BAND), :], kbuf[h >> group_shift, pl.ds(r0, kw), :], nt,
                             preferred_element_type=F32) for h in range(N_HEADS)], axis=0)
        s = s + bias_s[variant]
        m = jnp.max(s, axis=-1, keepdims=True)
        p = jnp.exp2(s - m)
        l = jnp.sum(p, axis=-1, keepdims=True)
        lse_tile = jnp.zeros((BAND, LANES), F32)
        ys = []
        for h in range(N_HEADS):
            rs = slice(h * BAND, (h + 1) * BAND)
            o = jnp.dot(p[rs].astype(BF16), vbuf[h >> group_shift, pl.ds(r0, kw), :],
                        preferred_element_type=F32)
            mh, lh = m[rs], l[rs]
            if sink:
                sk = sink_ref[h] * LOG2E
                mm = jnp.maximum(mh, sk)
                a = jnp.exp2(mh - mm)
                ys.append(o * a / (lh * a + jnp.exp2(sk - mm)))
            else:
                ys.append(o / lh)
                lse_tile = jnp.where(lane == h, mh + jnp.log2(lh), lse_tile)
        if native:
            o_ref[pl.ds(r0, BAND)] = _heads_to_native(ys)
        else:
            for h in range(N_HEADS):
                o_ref[pl.ds(r0, BAND), _head_cols(h)] = ys[h].astype(BF16)
        if not sink:
            lse_ref[pl.ds(r0, BAND), :] = lse_tile
        return carry

    lax.fori_loop(0, lc // BAND, block, 0)


def _band_chunks(t, dil):
    tl = t // dil
    lc = min(tl, 4 * BAND)
    assert t % dil == 0 and tl % lc == 0 and lc % BAND == 0
    return tl, lc


def _band_scratch(n_kv, lc, off):
    return [pltpu.VMEM((N_HEADS, lc, HEAD_DIM), BF16),
            pltpu.VMEM((n_kv, off + lc, HEAD_DIM), BF16), pltpu.VMEM((n_kv, off + lc, HEAD_DIM), BF16),
            pltpu.VMEM((2, N_HEADS * BAND, off + BAND), F32)]


def _band_attention_a(q, k_slots, v_slots, slot, dil):
    b, t, nh, e = q.shape
    n_slots = k_slots.shape[0]
    tl, lc = _band_chunks(t, dil)
    nc = tl // lc
    halo = nc > 1
    off = 0 if tl == BAND else BAND
    view = lambda a: a.reshape(b, tl, dil, nh, e)
    view_kv = lambda a: a.reshape(n_slots, b, tl, dil, nh, e)
    chunk = pl.BlockSpec((None, lc, None, nh, e), lambda bi, r, c: (bi, c, r, 0, 0))
    chunk_kv = pl.BlockSpec((None, None, lc, None, nh, e), lambda bi, r, c: (slot, bi, c, r, 0, 0))
    halo_kv = pl.BlockSpec((None, None, BAND, None, nh, e),
                           lambda bi, r, c: (slot, bi, jnp.maximum(c * (lc // BAND) - 1, 0), r, 0, 0))
    kv_specs = [halo_kv, chunk_kv] if halo else [chunk_kv]
    kv_args = lambda a: [view_kv(a)] * len(kv_specs)
    o, lse = pl.pallas_call(
        functools.partial(_band_kernel, n_kv=nh, lc=lc, dil=dil, sink=False, native=True, halo=halo),
        grid=(b, dil, nc),
        in_specs=[chunk] + kv_specs + kv_specs,
        out_specs=[chunk, pl.BlockSpec((None, lc, LANES), lambda bi, r, c: (bi, c, r))],
        out_shape=[jax.ShapeDtypeStruct((b, tl, dil, nh, e), F32),
                   jax.ShapeDtypeStruct((b, tl, dil * LANES), F32)],
        scratch_shapes=_band_scratch(nh, lc, off),
        compiler_params=_params("arbitrary", "arbitrary", "arbitrary"),
        name=f"band_d{dil}",
    )(view(q), *kv_args(k_slots), *kv_args(v_slots))
    return o.reshape(b, t, nh, e), lse.reshape(b, t, LANES)


def _band_attention_b(q, k, v, sinks):
    b, t, wq = q.shape
    wk = k.shape[2]
    tl, lc = _band_chunks(t, 1)
    nc = tl // lc
    halo = nc > 1
    off = 0 if tl == BAND else BAND
    chunk = lambda w: pl.BlockSpec((None, lc, w), lambda bi, r, c: (bi, c, 0))
    halo_kv = pl.BlockSpec((None, BAND, wk), lambda bi, r, c: (bi, jnp.maximum(c * (lc // BAND) - 1, 0), 0))
    kv_specs = [halo_kv, chunk(wk)] if halo else [chunk(wk)]
    n_kv = wk // HEAD_DIM
    return pl.pallas_call(
        functools.partial(_band_kernel, n_kv=n_kv, lc=lc, dil=1, sink=True, native=False, halo=halo),
        grid=(b, 1, nc),
        in_specs=[pl.BlockSpec(memory_space=pltpu.SMEM), chunk(wq)] + kv_specs + kv_specs,
        out_specs=chunk(wq),
        out_shape=jax.ShapeDtypeStruct((b, t, wq), BF16),
        scratch_shapes=_band_scratch(n_kv, lc, off),
        compiler_params=_params("arbitrary", "arbitrary", "arbitrary"),
        name="band_sink",
    )(sinks, q, *([k] * len(kv_specs)), *([v] * len(kv_specs)))


def _sample_key_table(past, st, n_far, n_near, n_keys, dilations):
    shape = (N_HEADS * SUBLANES, n_keys)
    rows = lax.broadcasted_iota(jnp.int32, shape, 0)
    j = lax.broadcasted_iota(jnp.int32, shape, 1)
    per_res = max(n_far // st, 1)
    far = (j % per_res) * MAX_DIL + j // per_res
    near = past - n_near + (j - n_far)
    n = jnp.where(j < n_far, far, near)
    dist = past + (rows & (SUBLANES - 1)) - n
    cnt = jnp.zeros(shape, F32)
    for dil in dilations:
        ok = (dist >= 0) & (dist <= BAND * dil) & ((dist & (dil - 1)) == 0)
        cnt = cnt + ok.astype(F32)
    slope2 = jnp.exp2(-((rows >> 3) + 1).astype(F32)) * LOG2E
    return cnt, jnp.where(cnt > 0.0, -slope2 * dist.astype(F32), NEG)


def _sample_softmax(q, kall, vall, cnt, bias):
    nt = (((1,), (1,)), ((), ()))
    group = N_HEADS // kall.shape[0]
    s = jnp.concatenate([lax.dot_general(q[h], kall[h // group], nt, preferred_element_type=F32)
                         for h in range(N_HEADS)], axis=0)
    s = s + bias
    m = jnp.max(s, axis=-1, keepdims=True)
    p = cnt * jnp.exp2(s - m)
    l = jnp.sum(p, axis=-1, keepdims=True)
    o = jnp.concatenate([jnp.dot(p[h * SUBLANES:(h + 1) * SUBLANES].astype(BF16), vall[h // group],
                                 preferred_element_type=F32) for h in range(N_HEADS)], axis=0)
    return m, l, o


def _sample_kernel(*refs, past_a, past_b, st):
    sink_ref, qa_ref, kan_ref, van_ref = refs[:4]
    kfar_refs, knear_ref = refs[4:4 + st], refs[4 + st]
    vfar_refs, vnear_ref = refs[5 + st:5 + 2 * st], refs[5 + 2 * st]
    (qb_ref, kbn_ref, vbn_ref, cbk_ref, cbv_ref, ya_ref, yb_ref,
     kall_a, vall_a, kall_b, vall_b, cnt_a, bias_a, cnt_b, bias_b) = refs[6 + 2 * st:]
    per_res = kfar_refs[0].shape[0]
    n_far = st * per_res
    n_near = knear_ref.shape[0]
    n_new = kan_ref.shape[0]
    used_a = n_far + n_near + n_new
    used_b = past_b + n_new

    @pl.when(pl.program_id(0) == 0)
    def _():
        for buf, used in ((kall_a, used_a), (vall_a, used_a), (kall_b, used_b), (vall_b, used_b)):
            buf[:, used:, :] = jnp.zeros((buf.shape[0], buf.shape[1] - used, buf.shape[2]), BF16)
        ca, ba = _sample_key_table(past_a, st, n_far, n_near, kall_a.shape[1], DILATIONS)
        cnt_a[...] = ca
        bias_a[...] = ba
        cb, bb = _sample_key_table(past_b, st, 0, past_b, kall_b.shape[1], (1,))
        cnt_b[...] = cb
        bias_b[...] = bb

    heads = lambda ref: _native_to_heads(ref[...]).astype(BF16)
    for far_refs, src_near, src_new, dst in ((kfar_refs, knear_ref, kan_ref, kall_a),
                                             (vfar_refs, vnear_ref, van_ref, vall_a)):
        for r, src_far in enumerate(far_refs):
            dst[:, r * per_res:(r + 1) * per_res, :] = heads(src_far)
        dst[:, n_far:n_far + n_near, :] = heads(src_near)
        dst[:, n_far + n_near:used_a, :] = heads(src_new)
    for src_old, src_new, dst in ((cbk_ref, kbn_ref, kall_b), (cbv_ref, vbn_ref, vall_b)):
        for h in range(N_KV_B):
            dst[h, 0:past_b, :] = src_old[:, h, :].astype(BF16)
            dst[h, past_b:used_b, :] = src_new[:, _head_cols(h)].astype(BF16)

    _, l, o = _sample_softmax(heads(qa_ref), kall_a, vall_a, cnt_a[...], bias_a[...])
    y = o / l
    for h in range(N_HEADS):
        ya_ref[:, _head_cols(h)] = y[h * SUBLANES:(h + 1) * SUBLANES]

    qb = jnp.stack([qb_ref[:, _head_cols(h)] for h in range(N_HEADS)], axis=0).astype(BF16)
    m, l, o = _sample_softmax(qb, kall_b, vall_b, cnt_b[...], bias_b[...])
    rows = lax.broadcasted_iota(jnp.int32, m.shape, 0)
    sk = jnp.zeros(m.shape, F32)
    for h in range(N_HEADS):
        sk = jnp.where(rows >> 3 == h, sink_ref[h] * LOG2E, sk)
    mm = jnp.maximum(m, sk)
    a = jnp.exp2(m - mm)
    y = o * a / (l * a + jnp.exp2(sk - mm))
    for h in range(N_HEADS):
        yb_ref[:, _head_cols(h)] = y[h * SUBLANES:(h + 1) * SUBLANES]


def _sample_attention(layer, st, sinks, qa, ka_new, va_new, cache_ak, cache_av, qb, kb_new, vb_new,
                      cache_bk, cache_bv):
    depth, b, past_a, nh, e = cache_ak.shape
    past_b, n_kv = cache_bk.shape[2], cache_bk.shape[3]
    n_new = ka_new.shape[1]
    wq = nh * e
    near_rows = BAND * DILATIONS[-2]
    assert past_a == BAND * MAX_DIL and st <= DILATIONS[-2] and past_b >= BAND and st <= SUBLANES
    assert past_b % BF16_ROWS == 0 and n_new % BF16_ROWS == 0
    n_far_blocks = (past_a - near_rows) // MAX_DIL
    n_keys_a = pl.cdiv(n_far_blocks * st + near_rows + n_new, LANES) * LANES
    n_keys_b = pl.cdiv(past_b + n_new, LANES) * LANES
    assert n_far_blocks % BF16_ROWS == 0 and near_rows % BF16_ROWS == 0 and past_a % near_rows == 0
    view = lambda c: c.reshape(depth, b, past_a // MAX_DIL, MAX_DIL, nh, e)
    far = [pl.BlockSpec((None, None, n_far_blocks, None, nh, e), lambda i, r=r: (layer, i, 0, r, 0, 0))
           for r in range(st)]
    near = pl.BlockSpec((None, None, near_rows, nh, e), lambda i: (layer, i, past_a // near_rows - 1, 0, 0))
    per_b = lambda *shape: pl.BlockSpec((None,) + shape, lambda i: (i,) + (0,) * len(shape))
    cache_b = pl.BlockSpec((None, None, past_b, n_kv, e), lambda i: (layer, i, 0, 0, 0))
    table = lambda n: pltpu.VMEM((N_HEADS * SUBLANES, n), F32)
    return pl.pallas_call(
        functools.partial(_sample_kernel, past_a=past_a, past_b=past_b, st=st),
        grid=(b,),
        in_specs=[pl.BlockSpec(memory_space=pltpu.SMEM),
                  per_b(SUBLANES, nh, e), per_b(n_new, nh, e), per_b(n_new, nh, e), *far, near, *far, near,
                  per_b(SUBLANES, wq), per_b(n_new, n_kv * e), per_b(n_new, n_kv * e), cache_b, cache_b],
        out_specs=[per_b(SUBLANES, wq), per_b(SUBLANES, wq)],
        out_shape=[jax.ShapeDtypeStruct((b, SUBLANES, wq), F32), jax.ShapeDtypeStruct((b, SUBLANES, wq), F32)],
        scratch_shapes=[pltpu.VMEM((nh, n_keys_a, e), BF16), pltpu.VMEM((nh, n_keys_a, e), BF16),
                        pltpu.VMEM((n_kv, n_keys_b, e), BF16), pltpu.VMEM((n_kv, n_keys_b, e), BF16),
                        table(n_keys_a), table(n_keys_a), table(n_keys_b), table(n_keys_b)],
        compiler_params=_params("arbitrary"),
        name="sample_attn",
    )(sinks, qa, ka_new, va_new, *([view(cache_ak)] * st), cache_ak, *([view(cache_av)] * st), cache_av,
      qb, kb_new, vb_new, cache_bk, cache_bv)


def _post_kernel(*refs, n_groups, tn):
    if n_groups:
        o_refs, refs = refs[:n_groups], refs[n_groups:]
        lse_refs, refs = refs[:n_groups], refs[n_groups:]
    else:
        ya_ref, refs = refs[0], refs[1:]
    yb_ref, ga_ref, gb_ref, x_ref, wba_ref, wbb_ref, wo_ref, out_ref, ya_s, mg_s = refs
    if n_groups:
        lses = [r[...] for r in lse_refs]
        os_ = [_native_to_heads(r[...]) for r in o_refs]
        for h in range(N_HEADS):
            e = [l[:, h:h + 1] for l in lses]
            m = functools.reduce(jnp.maximum, e)
            w = [jnp.exp2(ei - m) for ei in e]
            num = sum(wi * o[h] for wi, o in zip(w, os_))
            ya_s[:, _head_cols(h)] = (num / sum(w)).astype(BF16)
    else:
        ya_s[...] = ya_ref[...].astype(BF16)
    d = out_ref.shape[1]
    yb = yb_ref[...].astype(BF16)
    for n in range(d // tn):
        sl = slice(n * tn, (n + 1) * tn)
        ya_d = jnp.dot(ya_s[...], wba_ref[:, sl], preferred_element_type=F32)
        yb_d = jnp.dot(yb, wbb_ref[:, sl], preferred_element_type=F32)
        mg_s[:, sl] = (ga_ref[:, sl].astype(F32) * ya_d + gb_ref[:, sl].astype(F32) * yb_d).astype(BF16)
    for n in range(d // tn):
        sl = slice(n * tn, (n + 1) * tn)
        out_ref[:, sl] = x_ref[:, sl] + jnp.dot(mg_s[...], wo_ref[:, sl], preferred_element_type=F32)


def _post(ya_parts, yb, gates, x, wba, wbb, wo, tm):
    m, d = x.shape
    wa = wba.shape[0]
    rows = lambda w, col=0: pl.BlockSpec((tm, w), lambda i: (i, col))
    whole = lambda a: pl.BlockSpec(a.shape, lambda i: (0, 0), pipeline_mode=pl.Buffered(1))
    if isinstance(ya_parts, tuple):
        os_, lses = ya_parts
        n_groups = len(os_)
        ya_args = list(os_) + list(lses)
        native = pl.BlockSpec((tm, N_HEADS, HEAD_DIM), lambda i: (i, 0, 0))
        ya_specs = [native] * n_groups + [rows(LANES)] * n_groups
    else:
        n_groups = 0
        ya_args = [ya_parts]
        ya_specs = [rows(wa)]
    return pl.pallas_call(
        functools.partial(_post_kernel, n_groups=n_groups, tn=512),
        grid=(m // tm,),
        in_specs=ya_specs + [rows(wa), rows(d, 0), rows(d, 1), rows(d), whole(wba), whole(wbb), whole(wo)],
        out_specs=rows(d),
        out_shape=jax.ShapeDtypeStruct((m, d), F32),
        scratch_shapes=[pltpu.VMEM((tm, wa), BF16), pltpu.VMEM((tm, d), BF16)],
        compiler_params=_params("parallel"),
        name="post_attn",
    )(*ya_args, yb, gates, gates, x, wba, wbb, wo)


def _ffn_kernel(x_ref, g_ref, wg_ref, wu_ref, wd_ref, out_ref, h_ref):
    f = pl.program_id(1)

    @pl.when(f == 0)
    def _():
        x = x_ref[...]
        h_ref[...] = _rms_rows(x, g_ref[...]).astype(BF16)
        out_ref[...] = x

    h = h_ref[...]
    gate = jnp.dot(h, wg_ref[...], preferred_element_type=F32)
    up = jnp.dot(h, wu_ref[...], preferred_element_type=F32)
    u = (jax.nn.silu(gate) * up).astype(BF16)
    out_ref[...] += jnp.dot(u, wd_ref[...], preferred_element_type=F32)


def _ffn(x, g, wg, wu, wd, tm, tf):
    m, d = x.shape
    dff = wg.shape[1]
    assert m % tm == 0 and dff % tf == 0
    return pl.pallas_call(
        _ffn_kernel,
        grid=(m // tm, dff // tf),
        in_specs=[pl.BlockSpec((tm, d), lambda i, f: (i, 0)), pl.BlockSpec((1, d), lambda i, f: (0, 0)),
                  pl.BlockSpec((d, tf), lambda i, f: (0, f)), pl.BlockSpec((d, tf), lambda i, f: (0, f)),
                  pl.BlockSpec((tf, d), lambda i, f: (f, 0))],
        out_specs=pl.BlockSpec((tm, d), lambda i, f: (i, 0)),
        out_shape=jax.ShapeDtypeStruct((m, d), F32),
        scratch_shapes=[pltpu.VMEM((tm, d), BF16)],
        compiler_params=_params("parallel", "arbitrary"),
        name="ffn",
    )(x, g, wg, wu, wd)


def _pad_rows(a, n):
    return jnp.pad(a, ((0, 0), (0, n - a.shape[1])) + ((0, 0),) * (a.ndim - 2))


def kernel(x_prompt, x_sample, cache_a_k, cache_a_v, cache_b_k, cache_b_v, norm_mix, w_in, qnorm_a, knorm_a,
           qnorm_b, knorm_b, sinks_b, w_branch_a, w_branch_b, w_out, norm_ffn, w_ffn_gate, w_ffn_up, w_ffn_down):
    depth = w_in.shape[0]
    b, t, d = x_prompt.shape
    sb, st, _ = x_sample.shape
    wa = N_HEADS * HEAD_DIM
    keep_b = min(BAND, t)
    xp = x_prompt.reshape(b * t, d)
    xs = x_sample.reshape(sb * st, d)
    tm_p = 512
    tm_s = sb * st
    row = lambda v: v.reshape(1, -1)
    outs = [[] for _ in range(6)]
    k_slots = v_slots = None
    for l in range(depth):
        w_in_l = _pack_w_in(w_in[l], d)
        wba, wbb, wo = w_branch_a[l].astype(BF16), w_branch_b[l].astype(BF16), w_out[l].astype(BF16)
        wg, wu, wd = w_ffn_gate[l].astype(BF16), w_ffn_up[l].astype(BF16), w_ffn_down[l].astype(BF16)
        norms = (row(qnorm_a[l]), row(knorm_a[l]), row(qnorm_b[l]), row(knorm_b[l]))

        qa, k_slots, v_slots, qb, kb, vb, gates = _inproj(xp, row(norm_mix[l]), w_in_l, *norms, tm=tm_p,
                                                          prev=(k_slots, v_slots) if l else None)
        seq = lambda a: a.reshape((b, t) + a.shape[1:])
        per_seq = lambda a: a.reshape((l + 1, b, t) + a.shape[2:])
        parts = [_band_attention_a(seq(qa), per_seq(k_slots), per_seq(v_slots), l, dil) for dil in DILATIONS]
        os_ = [o.reshape(b * t, N_HEADS, HEAD_DIM) for o, _ in parts]
        lses = [s.reshape(b * t, LANES) for _, s in parts]
        yb = _band_attention_b(seq(qb), seq(kb), seq(vb), sinks_b[l]).reshape(b * t, wa)
        x1 = _post((os_, lses), yb, gates, xp, wba, wbb, wo, tm=256)
        xp = _ffn(x1, row(norm_ffn[l]), wg, wu, wd, tm=2 * tm_p, tf=512)
        outs[0].append(kb.reshape(b, t, N_KV_B, HEAD_DIM)[:, t - keep_b:])
        outs[1].append(vb.reshape(b, t, N_KV_B, HEAD_DIM)[:, t - keep_b:])

        qa, ka, va, qb, kb, vb, gates = _inproj(xs, row(norm_mix[l]), w_in_l, *norms, tm=tm_s)
        ka, va = ka[0], va[0]
        tok = lambda a: a.reshape((sb, st) + a.shape[1:])
        ya, yb = _sample_attention(
            l, st, sinks_b[l],
            _pad_rows(tok(qa), SUBLANES), _pad_rows(tok(ka), BF16_ROWS), _pad_rows(tok(va), BF16_ROWS),
            cache_a_k, cache_a_v,
            _pad_rows(tok(qb).astype(F32), SUBLANES), _pad_rows(tok(kb), BF16_ROWS), _pad_rows(tok(vb), BF16_ROWS),
            cache_b_k, cache_b_v)
        ya = ya[:, :st].reshape(sb * st, wa)
        yb = yb[:, :st].reshape(sb * st, wa)
        x1 = _post(ya, yb, gates, xs, wba, wbb, wo, tm=min(256, tm_s))
        xs = _ffn(x1, row(norm_ffn[l]), wg, wu, wd, tm=tm_s, tf=512)
        outs[2].append(tok(ka))
        outs[3].append(tok(va))
        outs[4].append(kb.reshape(sb, st, N_KV_B, HEAD_DIM))
        outs[5].append(vb.reshape(sb, st, N_KV_B, HEAD_DIM))
    cache_a = tuple(a.reshape(depth, b, t, N_HEADS, HEAD_DIM) for a in (k_slots, v_slots))
    return (xp.reshape(b, t, d), xs.reshape(sb, st, d)) + cache_a + tuple(jnp.stack(o) for o in outs)
```

```python
import functools
import math

import jax
import jax.numpy as jnp
from jax import lax
from jax.experimental import pallas as pl
from jax.experimental.pallas import tpu as pltpu

F32 = jnp.float32
BF16 = jnp.bfloat16

HEAD_DIM = 128
N_HEADS = 8
N_KV_B = 2
BAND = 128
DILATIONS = (1, 4, 16)
MAX_DIL = DILATIONS[-1]
EPS = 1e-6
NEG = -1e30
LOG2E = math.log2(math.e)
Q_SCALE = HEAD_DIM ** -0.5 * LOG2E
LANES = 128
SUBLANES = 8
BF16_ROWS = 16
MXU_COLS = 256
VMEM_LIMIT_BYTES = 56 * 1024 * 1024

TN_IN = 1024


def _slope2(h):
    return 2.0 ** (-(h + 1)) * LOG2E


def _params(*sem):
    return pltpu.CompilerParams(dimension_semantics=sem, vmem_limit_bytes=VMEM_LIMIT_BYTES)


def _rms_rows(x, gain):
    ms = jnp.mean(x * x, axis=-1, keepdims=True)
    return x * lax.rsqrt(ms + EPS) * gain


def _head_cols(h):
    return slice(h * HEAD_DIM, (h + 1) * HEAD_DIM)


def _heads_to_native(parts):
    return jnp.swapaxes(jnp.stack(parts, axis=0), 0, 1)


def _native_to_heads(x):
    return jnp.swapaxes(x, 0, 1)


def _inproj_kernel(*refs, n_prev):
    x_ref, g_ref, w_ref, qna_ref, kna_ref, qnb_ref, knb_ref = refs[:7]
    kprev_ref, vprev_ref = refs[7:9] if n_prev else (None, None)
    qa_ref, ka_ref, va_ref, qb_ref, kb_ref, vb_ref, gates_ref, h_ref = refs[9 if n_prev else 7:]
    j = pl.program_id(1)

    @pl.when(j == 0)
    def _():
        h_ref[...] = _rms_rows(x_ref[...], g_ref[...]).astype(BF16)

    def head_pairs(first=0, n=TN_IN // MXU_COLS):
        for c in range(first, first + n):
            p = jnp.dot(h_ref[...], w_ref[:, c * MXU_COLS:(c + 1) * MXU_COLS], preferred_element_type=F32)
            yield c, p[:, :HEAD_DIM], p[:, HEAD_DIM:]

    def normed_heads(gain):
        parts = []
        for _, lo, hi in head_pairs():
            parts += [_rms_rows(lo, gain), _rms_rows(hi, gain)]
        return parts

    @pl.when(j == 0)
    def _():
        qa_ref[...] = _heads_to_native(normed_heads(qna_ref[...] * Q_SCALE))

    @pl.when(j == 1)
    def _():
        ka_ref[n_prev] = _heads_to_native(normed_heads(kna_ref[...]))
        if n_prev:
            ka_ref[0:n_prev] = kprev_ref[...]

    @pl.when(j == 2)
    def _():
        parts = []
        for _, lo, hi in head_pairs():
            parts += [lo, hi]
        va_ref[n_prev] = _heads_to_native(parts)
        if n_prev:
            va_ref[0:n_prev] = vprev_ref[...]

    @pl.when(j == 3)
    def _():
        gain = qnb_ref[...] * Q_SCALE
        for c, lo, hi in head_pairs():
            qb_ref[:, _head_cols(2 * c)] = _rms_rows(lo, gain).astype(BF16)
            qb_ref[:, _head_cols(2 * c + 1)] = _rms_rows(hi, gain).astype(BF16)

    @pl.when(j == 4)
    def _():
        for _, lo, hi in head_pairs(0, 1):
            kb_ref[:, _head_cols(0)] = _rms_rows(lo, knb_ref[...])
            kb_ref[:, _head_cols(1)] = _rms_rows(hi, knb_ref[...])
        for _, lo, hi in head_pairs(1, 1):
            vb_ref[:, _head_cols(0)] = lo
            vb_ref[:, _head_cols(1)] = hi

    @pl.when(j >= 5)
    def _():
        for c, lo, hi in head_pairs():
            gates_ref[:, _head_cols(2 * c)] = (0.5 * jnp.tanh(0.5 * lo) + 0.5).astype(BF16)
            gates_ref[:, _head_cols(2 * c + 1)] = (0.5 * jnp.tanh(0.5 * hi) + 0.5).astype(BF16)


def _pack_w_in(w, d):
    wa = N_HEADS * HEAD_DIM
    kvw = N_KV_B * HEAD_DIM
    assert w.shape[1] == 4 * wa + 2 * kvw + 2 * d and kvw == MXU_COLS and wa == TN_IN
    pad = jnp.zeros((w.shape[0], TN_IN - 2 * kvw), w.dtype)
    return jnp.concatenate([w[:, :4 * wa + 2 * kvw], pad, w[:, 4 * wa + 2 * kvw:]], axis=1).astype(BF16)


def _inproj(x, g, w, qna, kna, qnb, knb, tm, prev=None):
    m, d = x.shape
    wa = N_HEADS * HEAD_DIM
    kvw = N_KV_B * HEAD_DIM
    assert m % tm == 0 and d % TN_IN == 0
    ng = 2 * d // TN_IN
    nj = 5 + ng
    assert w.shape == (d, nj * TN_IN)
    native = pl.BlockSpec((tm, N_HEADS, HEAD_DIM), lambda i, j: (i, 0, 0))
    flat = lambda width: pl.BlockSpec((tm, width), lambda i, j: (i, 0))
    row = lambda n: pl.BlockSpec((1, n), lambda i, j: (0, 0))
    n_prev = 0 if prev is None else prev[0].shape[0]
    slots = lambda n: pl.BlockSpec((n, tm, N_HEADS, HEAD_DIM), lambda i, j: (0, i, 0, 0))
    slots_shape = jax.ShapeDtypeStruct((n_prev + 1, m, N_HEADS, HEAD_DIM), F32)
    return pl.pallas_call(
        functools.partial(_inproj_kernel, n_prev=n_prev),
        grid=(m // tm, nj),
        in_specs=[pl.BlockSpec((tm, d), lambda i, j: (i, 0)), row(d),
                  pl.BlockSpec((d, TN_IN), lambda i, j: (0, j)),
                  row(HEAD_DIM), row(HEAD_DIM), row(HEAD_DIM), row(HEAD_DIM)] + [slots(n_prev)] * (2 if n_prev else 0),
        out_specs=[native, slots(n_prev + 1), slots(n_prev + 1), flat(wa), flat(kvw), flat(kvw),
                   pl.BlockSpec((tm, TN_IN), lambda i, j: (i, jnp.clip(j - 5, 0, ng - 1)))],
        out_shape=[jax.ShapeDtypeStruct((m, N_HEADS, HEAD_DIM), F32), slots_shape, slots_shape,
                   jax.ShapeDtypeStruct((m, wa), BF16),
                   jax.ShapeDtypeStruct((m, kvw), F32), jax.ShapeDtypeStruct((m, kvw), F32),
                   jax.ShapeDtypeStruct((m, 2 * d), BF16)],
        scratch_shapes=[pltpu.VMEM((tm, d), BF16)],
        compiler_params=_params("parallel", "arbitrary"),
        name="inproj",
    )(x, g, w, qna, kna, qnb, knb, *(prev or ()))


def _band_kernel(*refs, n_kv, lc, dil, sink, native, halo, cpb=1):
    refs = list(refs)
    sink_ref = refs.pop(0) if sink else None
    q_ref = refs.pop(0)
    kh_ref = refs.pop(0) if halo else None
    k_ref = refs.pop(0)
    vh_ref = refs.pop(0) if halo else None
    v_ref, o_ref = refs.pop(0), refs.pop(0)
    lse_ref = None if sink else refs.pop(0)
    qbuf, kbuf, vbuf, bias_s = refs
    c = pl.program_id(2)
    off = kbuf.shape[1] - lc
    kw = BAND + off

    @pl.when((pl.program_id(0) == 0) & (pl.program_id(1) == 0) & (c == 0))
    def _():
        rows = lax.broadcasted_iota(jnp.int32, (BAND, kw), 0)
        cols = lax.broadcasted_iota(jnp.int32, (BAND, kw), 1)
        back = rows + off - cols
        in_band = (back >= 0) & (back <= BAND)
        backf = back.astype(F32)
        for h in range(N_HEADS):
            bias = -(_slope2(h) * dil) * backf
            bias_s[0, h * BAND:(h + 1) * BAND, :] = jnp.where(in_band, bias, NEG)
            bias_s[1, h * BAND:(h + 1) * BAND, :] = jnp.where(in_band & (cols >= off), bias, NEG)
        if off and not halo:
            for buf in (kbuf, vbuf):
                buf[:, 0:off, :] = jnp.zeros((buf.shape[0], off, buf.shape[2]), BF16)

    def fill(cls):
        if native:
            rows_of = (lambda ref: ref[...]) if cpb == 1 else (lambda ref: ref[:, cls])
            heads = lambda ref: _native_to_heads(rows_of(ref)).astype(BF16)
            qbuf[...] = heads(q_ref)
            for buf, halo_ref, ref in ((kbuf, kh_ref, k_ref), (vbuf, vh_ref, v_ref)):
                if halo:
                    buf[:, 0:off, :] = heads(halo_ref)
                buf[:, off:, :] = heads(ref)
        else:
            for h in range(N_HEADS):
                qbuf[h] = q_ref[:, _head_cols(h)]
            for buf, halo_ref, ref in ((kbuf, kh_ref, k_ref), (vbuf, vh_ref, v_ref)):
                for h in range(n_kv):
                    if halo:
                        buf[h, 0:off, :] = halo_ref[:, _head_cols(h)].astype(BF16)
                    buf[h, off:, :] = ref[:, _head_cols(h)].astype(BF16)

    lane = lax.broadcasted_iota(jnp.int32, (BAND, LANES), 1)
    group_shift = (N_HEADS // n_kv).bit_length() - 1
    assert 1 << group_shift == N_HEADS // n_kv

    def block(i, cls):
        r0 = pl.multiple_of(i * BAND, BAND)
        variant = jnp.where((c == 0) & (i == 0), 1, 0)
        nt = (((1,), (1,)), ((), ()))
        s = jnp.concatenate(
            [lax.dot_general(qbuf[h, pl.ds(r0, BAND), :], kbuf[h >> group_shift, pl.ds(r0, kw), :], nt,
                             preferred_element_type=F32) for h in range(N_HEADS)], axis=0)
        s = s + bias_s[variant]
        m = jnp.max(s, axis=-1, keepdims=True)
        p = jnp.exp2(s - m)
        l = jnp.sum(p, axis=-1, keepdims=True)
        lse_tile = jnp.zeros((BAND, LANES), F32)
        ys = []
        for h in range(N_HEADS):
            rs = slice(h * BAND, (h + 1) * BAND)
            o = jnp.dot(p[rs].astype(BF16), vbuf[h >> group_shift, pl.ds(r0, kw), :],
                        preferred_element_type=F32)
            mh, lh = m[rs], l[rs]
            if sink:
                sk = sink_ref[h] * LOG2E
                mm = jnp.maximum(mh, sk)
                a = jnp.exp2(mh - mm)
                ys.append(o * a / (lh * a + jnp.exp2(sk - mm)))
            else:
                ys.append(o / lh)
                lse_tile = jnp.where(lane == h, mh + jnp.log2(lh), lse_tile)
        if native and cpb == 1:
            o_ref[pl.ds(r0, BAND)] = _heads_to_native(ys)
        elif native:
            o_ref[pl.ds(r0, BAND), cls] = _heads_to_native(ys)
        else:
            for h in range(N_HEADS):
                o_ref[pl.ds(r0, BAND), _head_cols(h)] = ys[h].astype(BF16)
        if not sink:
            lse_ref[pl.ds(r0, BAND), cls * LANES:(cls + 1) * LANES] = lse_tile

    for cls in range(cpb):
        fill(cls)
        lax.fori_loop(0, lc // BAND, lambda i, carry, cls=cls: (block(i, cls), carry)[1], 0)


def _band_chunks(t, dil):
    tl = t // dil
    lc = min(tl, 4 * BAND)
    assert t % dil == 0 and tl % lc == 0 and lc % BAND == 0
    return tl, lc


def _band_scratch(n_kv, lc, off):
    return [pltpu.VMEM((N_HEADS, lc, HEAD_DIM), BF16),
            pltpu.VMEM((n_kv, off + lc, HEAD_DIM), BF16), pltpu.VMEM((n_kv, off + lc, HEAD_DIM), BF16),
            pltpu.VMEM((2, N_HEADS * BAND, off + BAND), F32)]


def _band_attention_a(q, k_slots, v_slots, slot, dil):
    b, t, nh, e = q.shape
    n_slots = k_slots.shape[0]
    tl, lc = _band_chunks(t, dil)
    nc = tl // lc
    halo = nc > 1
    off = 0 if tl == BAND else BAND
    cpb = max(1, 4 * BAND // lc) if not halo else 1
    assert dil % cpb == 0
    cls_dim = None if cpb == 1 else cpb
    view = lambda a: a.reshape(b, tl, dil, nh, e)
    view_kv = lambda a: a.reshape(n_slots, b, tl, dil, nh, e)
    chunk = pl.BlockSpec((None, lc, cls_dim, nh, e), lambda bi, r, c: (bi, c, r, 0, 0))
    chunk_kv = pl.BlockSpec((None, None, lc, cls_dim, nh, e), lambda bi, r, c: (slot, bi, c, r, 0, 0))
    halo_kv = pl.BlockSpec((None, None, BAND, None, nh, e),
                           lambda bi, r, c: (slot, bi, jnp.maximum(c * (lc // BAND) - 1, 0), r, 0, 0))
    kv_specs = [halo_kv, chunk_kv] if halo else [chunk_kv]
    kv_args = lambda a: [view_kv(a)] * len(kv_specs)
    o, lse = pl.pallas_call(
        functools.partial(_band_kernel, n_kv=nh, lc=lc, dil=dil, sink=False, native=True, halo=halo, cpb=cpb),
        grid=(b, dil // cpb, nc),
        in_specs=[chunk] + kv_specs + kv_specs,
        out_specs=[chunk, pl.BlockSpec((None, lc, cpb * LANES), lambda bi, r, c: (bi, c, r))],
        out_shape=[jax.ShapeDtypeStruct((b, tl, dil, nh, e), F32),
                   jax.ShapeDtypeStruct((b, tl, dil * LANES), F32)],
        scratch_shapes=_band_scratch(nh, lc, off),
        compiler_params=_params("arbitrary", "arbitrary", "arbitrary"),
        name=f"band_d{dil}",
    )(view(q), *kv_args(k_slots), *kv_args(v_slots))
    return o.reshape(b, t, nh, e), lse.reshape(b, t, LANES)


def _band_attention_b(q, k, v, sinks):
    b, t, wq = q.shape
    wk = k.shape[2]
    tl, lc = _band_chunks(t, 1)
    nc = tl // lc
    halo = nc > 1
    off = 0 if tl == BAND else BAND
    chunk = lambda w: pl.BlockSpec((None, lc, w), lambda bi, r, c: (bi, c, 0))
    halo_kv = pl.BlockSpec((None, BAND, wk), lambda bi, r, c: (bi, jnp.maximum(c * (lc // BAND) - 1, 0), 0))
    kv_specs = [halo_kv, chunk(wk)] if halo else [chunk(wk)]
    n_kv = wk // HEAD_DIM
    return pl.pallas_call(
        functools.partial(_band_kernel, n_kv=n_kv, lc=lc, dil=1, sink=True, native=False, halo=halo),
        grid=(b, 1, nc),
        in_specs=[pl.BlockSpec(memory_space=pltpu.SMEM), chunk(wq)] + kv_specs + kv_specs,
        out_specs=chunk(wq),
        out_shape=jax.ShapeDtypeStruct((b, t, wq), BF16),
        scratch_shapes=_band_scratch(n_kv, lc, off),
        compiler_params=_params("arbitrary", "arbitrary", "arbitrary"),
        name="band_sink",
    )(sinks, q, *([k] * len(kv_specs)), *([v] * len(kv_specs)))


def _sample_key_table(past, st, n_far, n_near, n_cols, dilations, heads_per_key=1):
    shape = (N_HEADS * SUBLANES, n_cols)
    rows = lax.broadcasted_iota(jnp.int32, shape, 0)
    col = lax.broadcasted_iota(jnp.int32, shape, 1)
    j = col // heads_per_key
    per_res = max(n_far // st, 1)
    far = (j % per_res) * MAX_DIL + j // per_res
    near = past - n_near + (j - n_far)
    n = jnp.where(j < n_far, far, near)
    dist = past + (rows & (SUBLANES - 1)) - n
    cnt = jnp.zeros(shape, F32)
    for dil in dilations:
        ok = (dist >= 0) & (dist <= BAND * dil) & ((dist & (dil - 1)) == 0)
        cnt = cnt + ok.astype(F32)
    if heads_per_key > 1:
        cnt = jnp.where(col % heads_per_key == rows >> 3, cnt, 0.0)
    slope2 = jnp.exp2(-((rows >> 3) + 1).astype(F32)) * LOG2E
    return cnt, jnp.where(cnt > 0.0, -slope2 * dist.astype(F32), NEG)


def _sample_softmax(q, kall, vall, cnt, bias):
    nt = (((1,), (1,)), ((), ()))
    group = N_HEADS // kall.shape[0]
    s = jnp.concatenate([lax.dot_general(q[h], kall[h // group], nt, preferred_element_type=F32)
                         for h in range(N_HEADS)], axis=0)
    s = s + bias
    m = jnp.max(s, axis=-1, keepdims=True)
    p = cnt * jnp.exp2(s - m)
    l = jnp.sum(p, axis=-1, keepdims=True)
    o = jnp.concatenate([jnp.dot(p[h * SUBLANES:(h + 1) * SUBLANES].astype(BF16), vall[h // group],
                                 preferred_element_type=F32) for h in range(N_HEADS)], axis=0)
    return m, l, o


def _sample_kernel(*refs, past_a, past_b, st):
    sink_ref, qa_ref, kan_ref, van_ref = refs[:4]
    kfar_refs, knear_ref = refs[4:4 + st], refs[4 + st]
    vfar_refs, vnear_ref = refs[5 + st:5 + 2 * st], refs[5 + 2 * st]
    (qb_ref, kbn_ref, vbn_ref, cbk_ref, cbv_ref, ya_ref, yb_ref,
     kall_a, vall_a, kall_b, vall_b, bias_a, cnt_b, bias_b) = refs[6 + 2 * st:]
    per_res = kfar_refs[0].shape[0]
    n_far = st * per_res
    n_near = knear_ref.shape[0]
    n_new = kan_ref.shape[0]
    used_a = n_far + n_near + n_new
    used_b = past_b + n_new

    @pl.when(pl.program_id(0) == 0)
    def _():
        for buf, used in ((kall_a, used_a * N_HEADS), (vall_a, used_a * N_HEADS)):
            if buf.shape[0] > used:
                buf[used:, :] = jnp.zeros((buf.shape[0] - used, buf.shape[1]), BF16)
        for buf in (kall_b, vall_b):
            buf[:, used_b:, :] = jnp.zeros((buf.shape[0], buf.shape[1] - used_b, buf.shape[2]), BF16)
        ca, ba = _sample_key_table(past_a, st, n_far, n_near, kall_a.shape[0], DILATIONS, N_HEADS)
        bias_a[...] = ba + jnp.log2(jnp.maximum(ca, 1.0))
        cb, bb = _sample_key_table(past_b, st, 0, past_b, kall_b.shape[1], (1,))
        cnt_b[...] = cb
        bias_b[...] = bb

    rows2d = lambda ref: ref[...].reshape(ref.shape[0] * N_HEADS, HEAD_DIM).astype(BF16)
    for far_refs, src_near, src_new, dst in ((kfar_refs, knear_ref, kan_ref, kall_a),
                                             (vfar_refs, vnear_ref, van_ref, vall_a)):
        for r, src_far in enumerate(far_refs):
            dst[r * per_res * N_HEADS:(r + 1) * per_res * N_HEADS, :] = rows2d(src_far)
        dst[n_far * N_HEADS:(n_far + n_near) * N_HEADS, :] = rows2d(src_near)
        dst[(n_far + n_near) * N_HEADS:used_a * N_HEADS, :] = rows2d(src_new)
    for src_old, src_new, dst in ((cbk_ref, kbn_ref, kall_b), (cbv_ref, vbn_ref, vall_b)):
        for h in range(N_KV_B):
            dst[h, 0:past_b, :] = src_old[:, h, :].astype(BF16)
            dst[h, past_b:used_b, :] = src_new[:, _head_cols(h)].astype(BF16)

    qa = _native_to_heads(qa_ref[...]).reshape(N_HEADS * SUBLANES, HEAD_DIM).astype(BF16)
    s = lax.dot_general(qa, kall_a[...], (((1,), (1,)), ((), ())), preferred_element_type=F32) + bias_a[...]
    m = jnp.max(s, axis=-1, keepdims=True)
    p = jnp.exp2(s - m)
    l = jnp.sum(p, axis=-1, keepdims=True)
    y = jnp.dot(p.astype(BF16), vall_a[...], preferred_element_type=F32) / l
    for h in range(N_HEADS):
        ya_ref[:, _head_cols(h)] = y[h * SUBLANES:(h + 1) * SUBLANES]

    qb = jnp.stack([qb_ref[:, _head_cols(h)] for h in range(N_HEADS)], axis=0).astype(BF16)
    m, l, o = _sample_softmax(qb, kall_b, vall_b, cnt_b[...], bias_b[...])
    rows = lax.broadcasted_iota(jnp.int32, m.shape, 0)
    sk = jnp.zeros(m.shape, F32)
    for h in range(N_HEADS):
        sk = jnp.where(rows >> 3 == h, sink_ref[h] * LOG2E, sk)
    mm = jnp.maximum(m, sk)
    a = jnp.exp2(m - mm)
    y = o * a / (l * a + jnp.exp2(sk - mm))
    for h in range(N_HEADS):
        yb_ref[:, _head_cols(h)] = y[h * SUBLANES:(h + 1) * SUBLANES]


def _sample_attention(layer, st, sinks, qa, ka_new, va_new, cache_ak, cache_av, qb, kb_new, vb_new,
                      cache_bk, cache_bv):
    depth, b, past_a, nh, e = cache_ak.shape
    past_b, n_kv = cache_bk.shape[2], cache_bk.shape[3]
    n_new = ka_new.shape[1]
    wq = nh * e
    near_rows = BAND * DILATIONS[-2]
    assert past_a == BAND * MAX_DIL and st <= DILATIONS[-2] and past_b >= BAND and st <= SUBLANES
    assert past_b % BF16_ROWS == 0 and n_new % BF16_ROWS == 0
    n_far_blocks = (past_a - near_rows) // MAX_DIL
    n_cols_a = pl.cdiv((n_far_blocks * st + near_rows + n_new) * nh, LANES) * LANES
    n_keys_b = pl.cdiv(past_b + n_new, LANES) * LANES
    assert n_far_blocks % BF16_ROWS == 0 and near_rows % BF16_ROWS == 0 and past_a % near_rows == 0
    view = lambda c: c.reshape(depth, b, past_a // MAX_DIL, MAX_DIL, nh, e)
    far = [pl.BlockSpec((None, None, n_far_blocks, None, nh, e), lambda i, r=r: (layer, i, 0, r, 0, 0))
           for r in range(st)]
    near = pl.BlockSpec((None, None, near_rows, nh, e), lambda i: (layer, i, past_a // near_rows - 1, 0, 0))
    per_b = lambda *shape: pl.BlockSpec((None,) + shape, lambda i: (i,) + (0,) * len(shape))
    cache_b = pl.BlockSpec((None, None, past_b, n_kv, e), lambda i: (layer, i, 0, 0, 0))
    table = lambda n: pltpu.VMEM((N_HEADS * SUBLANES, n), F32)
    return pl.pallas_call(
        functools.partial(_sample_kernel, past_a=past_a, past_b=past_b, st=st),
        grid=(b,),
        in_specs=[pl.BlockSpec(memory_space=pltpu.SMEM),
                  per_b(SUBLANES, nh, e), per_b(n_new, nh, e), per_b(n_new, nh, e), *far, near, *far, near,
                  per_b(SUBLANES, wq), per_b(n_new, n_kv * e), per_b(n_new, n_kv * e), cache_b, cache_b],
        out_specs=[per_b(SUBLANES, wq), per_b(SUBLANES, wq)],
        out_shape=[jax.ShapeDtypeStruct((b, SUBLANES, wq), F32), jax.ShapeDtypeStruct((b, SUBLANES, wq), F32)],
        scratch_shapes=[pltpu.VMEM((n_cols_a, e), BF16), pltpu.VMEM((n_cols_a, e), BF16),
                        pltpu.VMEM((n_kv, n_keys_b, e), BF16), pltpu.VMEM((n_kv, n_keys_b, e), BF16),
                        table(n_cols_a), table(n_keys_b), table(n_keys_b)],
        compiler_params=_params("arbitrary"),
        name="sample_attn",
    )(sinks, qa, ka_new, va_new, *([view(cache_ak)] * st), cache_ak, *([view(cache_av)] * st), cache_av,
      qb, kb_new, vb_new, cache_bk, cache_bv)


def _merge_kernel(*refs, n_groups, tn):
    if n_groups:
        o_refs, refs = refs[:n_groups], refs[n_groups:]
        lse_refs, refs = refs[:n_groups], refs[n_groups:]
    else:
        ya_ref, refs = refs[0], refs[1:]
    yb_ref, ga_ref, gb_ref, wba_ref, wbb_ref, out_ref, ya_s = refs
    if n_groups:
        s = pl.program_id(0)

        @pl.when(s == 0)
        def _():
            ya_s[1] = jnp.zeros(ya_s.shape[1:], BF16)

        ya = ya_s[(s + 1) % 2]
        lses = [r[...] for r in lse_refs]
        os_ = [_native_to_heads(r[...]) for r in o_refs]
        nxt = s % 2
        for h in range(N_HEADS):
            e = [l[:, h:h + 1] for l in lses]
            m = functools.reduce(jnp.maximum, e)
            w = [jnp.exp2(ei - m) for ei in e]
            num = sum(wi * o[h] for wi, o in zip(w, os_))
            ya_s[nxt, :, _head_cols(h)] = (num / sum(w)).astype(BF16)
    else:
        ya = ya_ref[...].astype(BF16)
    yb = yb_ref[...].astype(BF16)
    for n in range(out_ref.shape[1] // tn):
        sl = slice(n * tn, (n + 1) * tn)
        ya_d = jnp.dot(ya, wba_ref[:, sl], preferred_element_type=F32)
        yb_d = jnp.dot(yb, wbb_ref[:, sl], preferred_element_type=F32)
        out_ref[:, sl] = (ga_ref[:, sl].astype(F32) * ya_d + gb_ref[:, sl].astype(F32) * yb_d).astype(BF16)


def _outproj_kernel(mg_ref, x_ref, wo_ref, out_ref, *, tn):
    for n in range(out_ref.shape[1] // tn):
        sl = slice(n * tn, (n + 1) * tn)
        out_ref[:, sl] = x_ref[:, sl] + jnp.dot(mg_ref[...], wo_ref[:, sl], preferred_element_type=F32)


def _post(ya_parts, yb, gates, x, wba, wbb, wo, tm_merge, tm_out):
    m, d = x.shape
    wa = wba.shape[0]
    whole = lambda a: pl.BlockSpec(a.shape, lambda i: (0, 0), pipeline_mode=pl.Buffered(1))
    n_blocks = m // tm_merge
    if isinstance(ya_parts, tuple):
        os_, lses = ya_parts
        n_groups = len(os_)
        ya_args = list(os_) + list(lses)
        steps = n_blocks + 1
        ahead = lambda s: jnp.minimum(s, n_blocks - 1)
        cur = lambda s: jnp.maximum(s - 1, 0)
        ya_specs = ([pl.BlockSpec((tm_merge, N_HEADS, HEAD_DIM), lambda s: (ahead(s), 0, 0))] * n_groups
                    + [pl.BlockSpec((tm_merge, LANES), lambda s: (ahead(s), 0))] * n_groups)
        scratch = [pltpu.VMEM((2, tm_merge, wa), BF16)]
        sem = "arbitrary"
    else:
        n_groups = 0
        ya_args = [ya_parts]
        steps = n_blocks
        cur = lambda s: s
        ya_specs = [pl.BlockSpec((tm_merge, wa), lambda s: (s, 0))]
        scratch = [pltpu.VMEM((SUBLANES, LANES), BF16)]
        sem = "parallel"
    rows = lambda w, col=0: pl.BlockSpec((tm_merge, w), lambda s: (cur(s), col))
    merged = pl.pallas_call(
        functools.partial(_merge_kernel, n_groups=n_groups, tn=512),
        grid=(steps,),
        in_specs=ya_specs + [rows(wa), rows(d, 0), rows(d, 1), whole(wba), whole(wbb)],
        out_specs=rows(d),
        out_shape=jax.ShapeDtypeStruct((m, d), BF16),
        scratch_shapes=scratch,
        compiler_params=_params(sem),
        name="merge_branches",
    )(*ya_args, yb, gates, gates, wba, wbb)
    rows_o = pl.BlockSpec((tm_out, d), lambda i: (i, 0))
    return pl.pallas_call(
        functools.partial(_outproj_kernel, tn=512),
        grid=(m // tm_out,),
        in_specs=[rows_o, rows_o, whole(wo)],
        out_specs=rows_o,
        out_shape=jax.ShapeDtypeStruct((m, d), F32),
        compiler_params=_params("parallel"),
        name="out_proj",
    )(merged, x, wo)


def _ffn_kernel(x_ref, g_ref, wg_ref, wu_ref, wd_ref, out_ref, h_ref):
    f = pl.program_id(1)

    @pl.when(f == 0)
    def _():
        x = x_ref[...]
        h_ref[...] = _rms_rows(x, g_ref[...]).astype(BF16)
        out_ref[...] = x

    h = h_ref[...]
    gate = jnp.dot(h, wg_ref[...], preferred_element_type=F32)
    up = jnp.dot(h, wu_ref[...], preferred_element_type=F32)
    u = (jax.nn.silu(gate) * up).astype(BF16)
    out_ref[...] += jnp.dot(u, wd_ref[...], preferred_element_type=F32)


def _ffn(x, g, wg, wu, wd, tm, tf):
    m, d = x.shape
    dff = wg.shape[1]
    assert m % tm == 0 and dff % tf == 0
    return pl.pallas_call(
        _ffn_kernel,
        grid=(m // tm, dff // tf),
        in_specs=[pl.BlockSpec((tm, d), lambda i, f: (i, 0)), pl.BlockSpec((1, d), lambda i, f: (0, 0)),
                  pl.BlockSpec((d, tf), lambda i, f: (0, f)), pl.BlockSpec((d, tf), lambda i, f: (0, f)),
                  pl.BlockSpec((tf, d), lambda i, f: (f, 0))],
        out_specs=pl.BlockSpec((tm, d), lambda i, f: (i, 0)),
        out_shape=jax.ShapeDtypeStruct((m, d), F32),
        scratch_shapes=[pltpu.VMEM((tm, d), BF16)],
        compiler_params=_params("parallel", "arbitrary"),
        name="ffn",
    )(x, g, wg, wu, wd)


def _pad_rows(a, n):
    return jnp.pad(a, ((0, 0), (0, n - a.shape[1])) + ((0, 0),) * (a.ndim - 2))


def kernel(x_prompt, x_sample, cache_a_k, cache_a_v, cache_b_k, cache_b_v, norm_mix, w_in, qnorm_a, knorm_a,
           qnorm_b, knorm_b, sinks_b, w_branch_a, w_branch_b, w_out, norm_ffn, w_ffn_gate, w_ffn_up, w_ffn_down):
    depth = w_in.shape[0]
    b, t, d = x_prompt.shape
    sb, st, _ = x_sample.shape
    wa = N_HEADS * HEAD_DIM
    keep_b = min(BAND, t)
    xp = x_prompt.reshape(b * t, d)
    xs = x_sample.reshape(sb * st, d)
    tm_p = 512
    tm_s = sb * st
    row = lambda v: v.reshape(1, -1)
    outs = [[] for _ in range(6)]
    k_slots = v_slots = None
    for l in range(depth):
        w_in_l = _pack_w_in(w_in[l], d)
        wba, wbb, wo = w_branch_a[l].astype(BF16), w_branch_b[l].astype(BF16), w_out[l].astype(BF16)
        wg, wu, wd = w_ffn_gate[l].astype(BF16), w_ffn_up[l].astype(BF16), w_ffn_down[l].astype(BF16)
        norms = (row(qnorm_a[l]), row(knorm_a[l]), row(qnorm_b[l]), row(knorm_b[l]))

        qa, k_slots, v_slots, qb, kb, vb, gates = _inproj(xp, row(norm_mix[l]), w_in_l, *norms, tm=tm_p,
                                                          prev=(k_slots, v_slots) if l else None)
        seq = lambda a: a.reshape((b, t) + a.shape[1:])
        per_seq = lambda a: a.reshape((l + 1, b, t) + a.shape[2:])
        parts = [_band_attention_a(seq(qa), per_seq(k_slots), per_seq(v_slots), l, dil) for dil in DILATIONS]
        os_ = [o.reshape(b * t, N_HEADS, HEAD_DIM) for o, _ in parts]
        lses = [s.reshape(b * t, LANES) for _, s in parts]
        yb = _band_attention_b(seq(qb), seq(kb), seq(vb), sinks_b[l]).reshape(b * t, wa)
        x1 = _post((os_, lses), yb, gates, xp, wba, wbb, wo, tm_merge=tm_p, tm_out=2 * tm_p)
        xp = _ffn(x1, row(norm_ffn[l]), wg, wu, wd, tm=2 * tm_p, tf=512)
        outs[0].append(kb.reshape(b, t, N_KV_B, HEAD_DIM)[:, t - keep_b:])
        outs[1].append(vb.reshape(b, t, N_KV_B, HEAD_DIM)[:, t - keep_b:])

        qa, ka, va, qb, kb, vb, gates = _inproj(xs, row(norm_mix[l]), w_in_l, *norms, tm=tm_s)
        ka, va = ka[0], va[0]
        tok = lambda a: a.reshape((sb, st) + a.shape[1:])
        ya, yb = _sample_attention(
            l, st, sinks_b[l],
            _pad_rows(tok(qa), SUBLANES), _pad_rows(tok(ka), BF16_ROWS), _pad_rows(tok(va), BF16_ROWS),
            cache_a_k, cache_a_v,
            _pad_rows(tok(qb).astype(F32), SUBLANES), _pad_rows(tok(kb), BF16_ROWS), _pad_rows(tok(vb), BF16_ROWS),
            cache_b_k, cache_b_v)
        ya = ya[:, :st].reshape(sb * st, wa)
        yb = yb[:, :st].reshape(sb * st, wa)
        x1 = _post(ya, yb, gates, xs, wba, wbb, wo, tm_merge=tm_s, tm_out=tm_s)
        xs = _ffn(x1, row(norm_ffn[l]), wg, wu, wd, tm=tm_s, tf=512)
        outs[2].append(tok(ka))
        outs[3].append(tok(va))
        outs[4].append(kb.reshape(sb, st, N_KV_B, HEAD_DIM))
        outs[5].append(vb.reshape(sb, st, N_KV_B, HEAD_DIM))
    cache_a = tuple(a.reshape(depth, b, t, N_HEADS, HEAD_DIM) for a in (k_slots, v_slots))
    return (xp.reshape(b, t, d), xs.reshape(sb, st, d)) + cache_a + tuple(jnp.stack(o) for o in outs)
```

```python
import functools
import math

import jax
import jax.numpy as jnp
from jax import lax
from jax.experimental import pallas as pl
from jax.experimental.pallas import tpu as pltpu

F32 = jnp.float32
BF16 = jnp.bfloat16

HEAD_DIM = 128
N_HEADS = 8
N_KV_B = 2
BAND = 128
DILATIONS = (1, 4, 16)
MAX_DIL = DILATIONS[-1]
EPS = 1e-6
NEG = -1e30
LOG2E = math.log2(math.e)
Q_SCALE = HEAD_DIM ** -0.5 * LOG2E
LANES = 128
SUBLANES = 8
BF16_ROWS = 16
MXU_COLS = 256
VMEM_LIMIT_BYTES = 56 * 1024 * 1024

TN_IN = 1024


def _slope2(h):
    return 2.0 ** (-(h + 1)) * LOG2E


def _params(*sem):
    return pltpu.CompilerParams(dimension_semantics=sem, vmem_limit_bytes=VMEM_LIMIT_BYTES)


def _rms_rows(x, gain):
    ms = jnp.mean(x * x, axis=-1, keepdims=True)
    return x * lax.rsqrt(ms + EPS) * gain


def _head_cols(h):
    return slice(h * HEAD_DIM, (h + 1) * HEAD_DIM)


def _heads_to_native(parts):
    return jnp.swapaxes(jnp.stack(parts, axis=0), 0, 1)


def _native_to_heads(x):
    return jnp.swapaxes(x, 0, 1)


def _inproj_kernel(*refs, n_prev):
    x_ref, g_ref, w_ref, qna_ref, kna_ref, qnb_ref, knb_ref = refs[:7]
    kprev_ref, vprev_ref = refs[7:9] if n_prev else (None, None)
    qa_ref, ka_ref, va_ref, qb_ref, kb_ref, vb_ref, gates_ref, h_ref = refs[9 if n_prev else 7:]
    j = pl.program_id(1)

    @pl.when(j == 0)
    def _():
        h_ref[...] = _rms_rows(x_ref[...], g_ref[...]).astype(BF16)

    def head_pairs(first=0, n=TN_IN // MXU_COLS):
        for c in range(first, first + n):
            p = jnp.dot(h_ref[...], w_ref[:, c * MXU_COLS:(c + 1) * MXU_COLS], preferred_element_type=F32)
            yield c, p[:, :HEAD_DIM], p[:, HEAD_DIM:]

    def normed_heads(gain):
        parts = []
        for _, lo, hi in head_pairs():
            parts += [_rms_rows(lo, gain), _rms_rows(hi, gain)]
        return parts

    @pl.when(j == 0)
    def _():
        qa_ref[...] = _heads_to_native(normed_heads(qna_ref[...] * Q_SCALE))

    @pl.when(j == 1)
    def _():
        ka_ref[n_prev] = _heads_to_native(normed_heads(kna_ref[...]))
        if n_prev:
            ka_ref[0:n_prev] = kprev_ref[...]

    @pl.when(j == 2)
    def _():
        parts = []
        for _, lo, hi in head_pairs():
            parts += [lo, hi]
        va_ref[n_prev] = _heads_to_native(parts)
        if n_prev:
            va_ref[0:n_prev] = vprev_ref[...]

    @pl.when(j == 3)
    def _():
        gain = qnb_ref[...] * Q_SCALE
        for c, lo, hi in head_pairs():
            qb_ref[:, _head_cols(2 * c)] = _rms_rows(lo, gain).astype(BF16)
            qb_ref[:, _head_cols(2 * c + 1)] = _rms_rows(hi, gain).astype(BF16)

    @pl.when(j == 4)
    def _():
        for _, lo, hi in head_pairs(0, 1):
            kb_ref[:, _head_cols(0)] = _rms_rows(lo, knb_ref[...])
            kb_ref[:, _head_cols(1)] = _rms_rows(hi, knb_ref[...])
        for _, lo, hi in head_pairs(1, 1):
            vb_ref[:, _head_cols(0)] = lo
            vb_ref[:, _head_cols(1)] = hi

    @pl.when(j >= 5)
    def _():
        for c, lo, hi in head_pairs():
            gates_ref[:, _head_cols(2 * c)] = (0.5 * jnp.tanh(0.5 * lo) + 0.5).astype(BF16)
            gates_ref[:, _head_cols(2 * c + 1)] = (0.5 * jnp.tanh(0.5 * hi) + 0.5).astype(BF16)


def _pack_w_in(w, d):
    wa = N_HEADS * HEAD_DIM
    kvw = N_KV_B * HEAD_DIM
    assert w.shape[1] == 4 * wa + 2 * kvw + 2 * d and kvw == MXU_COLS and wa == TN_IN
    pad = jnp.zeros((w.shape[0], TN_IN - 2 * kvw), w.dtype)
    return jnp.concatenate([w[:, :4 * wa + 2 * kvw], pad, w[:, 4 * wa + 2 * kvw:]], axis=1).astype(BF16)


def _inproj(x, g, w, qna, kna, qnb, knb, tm, prev=None):
    m, d = x.shape
    wa = N_HEADS * HEAD_DIM
    kvw = N_KV_B * HEAD_DIM
    assert m % tm == 0 and d % TN_IN == 0
    ng = 2 * d // TN_IN
    nj = 5 + ng
    assert w.shape == (d, nj * TN_IN)
    native = pl.BlockSpec((tm, N_HEADS, HEAD_DIM), lambda i, j: (i, 0, 0))
    flat = lambda width: pl.BlockSpec((tm, width), lambda i, j: (i, 0))
    row = lambda n: pl.BlockSpec((1, n), lambda i, j: (0, 0))
    n_prev = 0 if prev is None else prev[0].shape[0]
    slots = lambda n: pl.BlockSpec((n, tm, N_HEADS, HEAD_DIM), lambda i, j: (0, i, 0, 0))
    slots_shape = jax.ShapeDtypeStruct((n_prev + 1, m, N_HEADS, HEAD_DIM), F32)
    return pl.pallas_call(
        functools.partial(_inproj_kernel, n_prev=n_prev),
        grid=(m // tm, nj),
        in_specs=[pl.BlockSpec((tm, d), lambda i, j: (i, 0)), row(d),
                  pl.BlockSpec((d, TN_IN), lambda i, j: (0, j)),
                  row(HEAD_DIM), row(HEAD_DIM), row(HEAD_DIM), row(HEAD_DIM)] + [slots(n_prev)] * (2 if n_prev else 0),
        out_specs=[native, slots(n_prev + 1), slots(n_prev + 1), flat(wa), flat(kvw), flat(kvw),
                   pl.BlockSpec((tm, TN_IN), lambda i, j: (i, jnp.clip(j - 5, 0, ng - 1)))],
        out_shape=[jax.ShapeDtypeStruct((m, N_HEADS, HEAD_DIM), F32), slots_shape, slots_shape,
                   jax.ShapeDtypeStruct((m, wa), BF16),
                   jax.ShapeDtypeStruct((m, kvw), F32), jax.ShapeDtypeStruct((m, kvw), F32),
                   jax.ShapeDtypeStruct((m, 2 * d), BF16)],
        scratch_shapes=[pltpu.VMEM((tm, d), BF16)],
        compiler_params=_params("parallel", "arbitrary"),
        name="inproj",
    )(x, g, w, qna, kna, qnb, knb, *(prev or ()))


def _band_kernel(*refs, n_kv, lc, dil, sink, native, halo, cpb=1):
    refs = list(refs)
    sink_ref = refs.pop(0) if sink else None
    q_ref = refs.pop(0)
    kh_ref = refs.pop(0) if halo else None
    k_ref = refs.pop(0)
    vh_ref = refs.pop(0) if halo else None
    v_ref, o_ref = refs.pop(0), refs.pop(0)
    lse_ref = None if sink else refs.pop(0)
    qbuf, kbuf, vbuf, bias_s = refs
    c = pl.program_id(2)
    off = kbuf.shape[1] - lc
    kw = BAND + off

    @pl.when((pl.program_id(0) == 0) & (pl.program_id(1) == 0) & (c == 0))
    def _():
        rows = lax.broadcasted_iota(jnp.int32, (BAND, kw), 0)
        cols = lax.broadcasted_iota(jnp.int32, (BAND, kw), 1)
        back = rows + off - cols
        in_band = (back >= 0) & (back <= BAND)
        backf = back.astype(F32)
        for h in range(N_HEADS):
            bias = -(_slope2(h) * dil) * backf
            bias_s[0, h * BAND:(h + 1) * BAND, :] = jnp.where(in_band, bias, NEG)
            bias_s[1, h * BAND:(h + 1) * BAND, :] = jnp.where(in_band & (cols >= off), bias, NEG)
        if off and not halo:
            for buf in (kbuf, vbuf):
                buf[:, 0:off, :] = jnp.zeros((buf.shape[0], off, buf.shape[2]), BF16)

    def fill(cls):
        if native:
            rows_of = (lambda ref: ref[...]) if cpb == 1 else (lambda ref: ref[:, cls])
            heads = lambda ref: _native_to_heads(rows_of(ref)).astype(BF16)
            qbuf[...] = heads(q_ref)
            for buf, halo_ref, ref in ((kbuf, kh_ref, k_ref), (vbuf, vh_ref, v_ref)):
                if halo:
                    buf[:, 0:off, :] = heads(halo_ref)
                buf[:, off:, :] = heads(ref)
        else:
            for h in range(N_HEADS):
                qbuf[h] = q_ref[:, _head_cols(h)]
            for buf, halo_ref, ref in ((kbuf, kh_ref, k_ref), (vbuf, vh_ref, v_ref)):
                for h in range(n_kv):
                    if halo:
                        buf[h, 0:off, :] = halo_ref[:, _head_cols(h)].astype(BF16)
                    buf[h, off:, :] = ref[:, _head_cols(h)].astype(BF16)

    lane = lax.broadcasted_iota(jnp.int32, (BAND, LANES), 1)
    group_shift = (N_HEADS // n_kv).bit_length() - 1
    assert 1 << group_shift == N_HEADS // n_kv

    def block(i, cls):
        r0 = pl.multiple_of(i * BAND, BAND)
        variant = jnp.where((c == 0) & (i == 0), 1, 0)
        nt = (((1,), (1,)), ((), ()))
        s = jnp.concatenate(
            [lax.dot_general(qbuf[h, pl.ds(r0, BAND), :], kbuf[h >> group_shift, pl.ds(r0, kw), :], nt,
                             preferred_element_type=F32) for h in range(N_HEADS)], axis=0)
        s = s + bias_s[variant]
        m = jnp.max(s, axis=-1, keepdims=True)
        p = jnp.exp2(s - m)
        l = jnp.sum(p, axis=-1, keepdims=True)
        lse_tile = jnp.zeros((BAND, LANES), F32)
        ys = []
        for h in range(N_HEADS):
            rs = slice(h * BAND, (h + 1) * BAND)
            o = jnp.dot(p[rs].astype(BF16), vbuf[h >> group_shift, pl.ds(r0, kw), :],
                        preferred_element_type=F32)
            mh, lh = m[rs], l[rs]
            if sink:
                sk = sink_ref[h] * LOG2E
                mm = jnp.maximum(mh, sk)
                a = jnp.exp2(mh - mm)
                ys.append(o * a / (lh * a + jnp.exp2(sk - mm)))
            else:
                ys.append(o / lh)
                lse_tile = jnp.where(lane == h, mh + jnp.log2(lh), lse_tile)
        if native and cpb == 1:
            o_ref[pl.ds(r0, BAND)] = _heads_to_native(ys)
        elif native:
            o_ref[pl.ds(r0, BAND), cls] = _heads_to_native(ys)
        else:
            for h in range(N_HEADS):
                o_ref[pl.ds(r0, BAND), _head_cols(h)] = ys[h].astype(BF16)
        if not sink:
            lse_ref[pl.ds(r0, BAND), cls * LANES:(cls + 1) * LANES] = lse_tile

    for cls in range(cpb):
        fill(cls)
        lax.fori_loop(0, lc // BAND, lambda i, carry, cls=cls: (block(i, cls), carry)[1], 0)


def _band_chunks(t, dil):
    tl = t // dil
    lc = min(tl, 4 * BAND)
    assert t % dil == 0 and tl % lc == 0 and lc % BAND == 0
    return tl, lc


def _band_scratch(n_kv, lc, off):
    return [pltpu.VMEM((N_HEADS, lc, HEAD_DIM), BF16),
            pltpu.VMEM((n_kv, off + lc, HEAD_DIM), BF16), pltpu.VMEM((n_kv, off + lc, HEAD_DIM), BF16),
            pltpu.VMEM((2, N_HEADS * BAND, off + BAND), F32)]


def _band_attention_a(q, k_slots, v_slots, slot, dil):
    b, t, nh, e = q.shape
    n_slots = k_slots.shape[0]
    tl, lc = _band_chunks(t, dil)
    nc = tl // lc
    halo = nc > 1
    off = 0 if tl == BAND else BAND
    cpb = max(1, 4 * BAND // lc) if not halo else 1
    assert dil % cpb == 0
    cls_dim = None if cpb == 1 else cpb
    view = lambda a: a.reshape(b, tl, dil, nh, e)
    view_kv = lambda a: a.reshape(n_slots, b, tl, dil, nh, e)
    chunk = pl.BlockSpec((None, lc, cls_dim, nh, e), lambda bi, r, c: (bi, c, r, 0, 0))
    chunk_kv = pl.BlockSpec((None, None, lc, cls_dim, nh, e), lambda bi, r, c: (slot, bi, c, r, 0, 0))
    halo_kv = pl.BlockSpec((None, None, BAND, None, nh, e),
                           lambda bi, r, c: (slot, bi, jnp.maximum(c * (lc // BAND) - 1, 0), r, 0, 0))
    kv_specs = [halo_kv, chunk_kv] if halo else [chunk_kv]
    kv_args = lambda a: [view_kv(a)] * len(kv_specs)
    o, lse = pl.pallas_call(
        functools.partial(_band_kernel, n_kv=nh, lc=lc, dil=dil, sink=False, native=True, halo=halo, cpb=cpb),
        grid=(b, dil // cpb, nc),
        in_specs=[chunk] + kv_specs + kv_specs,
        out_specs=[chunk, pl.BlockSpec((None, lc, cpb * LANES), lambda bi, r, c: (bi, c, r))],
        out_shape=[jax.ShapeDtypeStruct((b, tl, dil, nh, e), F32),
                   jax.ShapeDtypeStruct((b, tl, dil * LANES), F32)],
        scratch_shapes=_band_scratch(nh, lc, off),
        compiler_params=_params("arbitrary", "arbitrary", "arbitrary"),
        name=f"band_d{dil}",
    )(view(q), *kv_args(k_slots), *kv_args(v_slots))
    return o.reshape(b, t, nh, e), lse.reshape(b, t, LANES)


def _band_attention_b(q, k, v, sinks):
    b, t, wq = q.shape
    wk = k.shape[2]
    tl, lc = _band_chunks(t, 1)
    nc = tl // lc
    halo = nc > 1
    off = 0 if tl == BAND else BAND
    chunk = lambda w: pl.BlockSpec((None, lc, w), lambda bi, r, c: (bi, c, 0))
    halo_kv = pl.BlockSpec((None, BAND, wk), lambda bi, r, c: (bi, jnp.maximum(c * (lc // BAND) - 1, 0), 0))
    kv_specs = [halo_kv, chunk(wk)] if halo else [chunk(wk)]
    n_kv = wk // HEAD_DIM
    return pl.pallas_call(
        functools.partial(_band_kernel, n_kv=n_kv, lc=lc, dil=1, sink=True, native=False, halo=halo),
        grid=(b, 1, nc),
        in_specs=[pl.BlockSpec(memory_space=pltpu.SMEM), chunk(wq)] + kv_specs + kv_specs,
        out_specs=chunk(wq),
        out_shape=jax.ShapeDtypeStruct((b, t, wq), BF16),
        scratch_shapes=_band_scratch(n_kv, lc, off),
        compiler_params=_params("arbitrary", "arbitrary", "arbitrary"),
        name="band_sink",
    )(sinks, q, *([k] * len(kv_specs)), *([v] * len(kv_specs)))


def _sample_key_table(past, st, n_far, n_near, n_cols, dilations, heads_per_key=1):
    shape = (N_HEADS * SUBLANES, n_cols)
    rows = lax.broadcasted_iota(jnp.int32, shape, 0)
    col = lax.broadcasted_iota(jnp.int32, shape, 1)
    j = col // heads_per_key
    far = (j // st) * MAX_DIL + j % st
    near = past - n_near + (j - n_far)
    n = jnp.where(j < n_far, far, near)
    dist = past + (rows & (SUBLANES - 1)) - n
    cnt = jnp.zeros(shape, F32)
    for dil in dilations:
        ok = (dist >= 0) & (dist <= BAND * dil) & ((dist & (dil - 1)) == 0)
        cnt = cnt + ok.astype(F32)
    if heads_per_key > 1:
        cnt = jnp.where(col % heads_per_key == rows >> 3, cnt, 0.0)
    slope2 = jnp.exp2(-((rows >> 3) + 1).astype(F32)) * LOG2E
    return cnt, jnp.where(cnt > 0.0, -slope2 * dist.astype(F32), NEG)


def _sample_softmax(q, kall, vall, cnt, bias):
    nt = (((1,), (1,)), ((), ()))
    group = N_HEADS // kall.shape[0]
    s = jnp.concatenate([lax.dot_general(q[h], kall[h // group], nt, preferred_element_type=F32)
                         for h in range(N_HEADS)], axis=0)
    s = s + bias
    m = jnp.max(s, axis=-1, keepdims=True)
    p = cnt * jnp.exp2(s - m)
    l = jnp.sum(p, axis=-1, keepdims=True)
    o = jnp.concatenate([jnp.dot(p[h * SUBLANES:(h + 1) * SUBLANES].astype(BF16), vall[h // group],
                                 preferred_element_type=F32) for h in range(N_HEADS)], axis=0)
    return m, l, o


def _sample_kernel(sink_ref, qa_ref, kan_ref, van_ref, kfar_ref, knear_ref, vfar_ref, vnear_ref,
                   qb_ref, kbn_ref, vbn_ref, cbk_ref, cbv_ref, ya_ref, yb_ref,
                   kall_a, vall_a, kall_b, vall_b, bias_a, cnt_b, bias_b, *, past_a, past_b, st):
    n_far = kfar_ref.shape[1] * kfar_ref.shape[2]
    n_near = knear_ref.shape[1]
    n_new = kan_ref.shape[1]
    used_a = n_far + n_near + n_new
    used_b = past_b + n_new

    @pl.when(pl.program_id(0) == 0)
    def _():
        for buf, used in ((kall_a, used_a * N_HEADS), (vall_a, used_a * N_HEADS)):
            if buf.shape[0] > used:
                buf[used:, :] = jnp.zeros((buf.shape[0] - used, buf.shape[1]), BF16)
        for buf in (kall_b, vall_b):
            buf[:, used_b:, :] = jnp.zeros((buf.shape[0], buf.shape[1] - used_b, buf.shape[2]), BF16)
        ca, ba = _sample_key_table(past_a, st, n_far, n_near, kall_a.shape[0], DILATIONS, N_HEADS)
        bias_a[...] = ba + jnp.log2(jnp.maximum(ca, 1.0))
        cb, bb = _sample_key_table(past_b, st, 0, past_b, kall_b.shape[1], (1,))
        cnt_b[...] = cb
        bias_b[...] = bb

    rows2d = lambda x: x.reshape(-1, HEAD_DIM).astype(BF16)
    for bi in range(qa_ref.shape[0]):
        for src_far, src_near, src_new, dst in ((kfar_ref, knear_ref, kan_ref, kall_a),
                                                (vfar_ref, vnear_ref, van_ref, vall_a)):
            dst[0:n_far * N_HEADS, :] = rows2d(src_far[bi])
            dst[n_far * N_HEADS:(n_far + n_near) * N_HEADS, :] = rows2d(src_near[bi])
            dst[(n_far + n_near) * N_HEADS:used_a * N_HEADS, :] = rows2d(src_new[bi])
        for src_old, src_new, dst in ((cbk_ref, kbn_ref, kall_b), (cbv_ref, vbn_ref, vall_b)):
            for h in range(N_KV_B):
                dst[h, 0:past_b, :] = src_old[bi, :, h, :].astype(BF16)
                dst[h, past_b:used_b, :] = src_new[bi, :, _head_cols(h)].astype(BF16)

        qa = _native_to_heads(qa_ref[bi]).reshape(N_HEADS * SUBLANES, HEAD_DIM).astype(BF16)
        s = lax.dot_general(qa, kall_a[...], (((1,), (1,)), ((), ())), preferred_element_type=F32) + bias_a[...]
        m = jnp.max(s, axis=-1, keepdims=True)
        p = jnp.exp2(s - m)
        l = jnp.sum(p, axis=-1, keepdims=True)
        y = jnp.dot(p.astype(BF16), vall_a[...], preferred_element_type=F32) / l
        for h in range(N_HEADS):
            ya_ref[bi, :, _head_cols(h)] = y[h * SUBLANES:(h + 1) * SUBLANES]

        qb = jnp.stack([qb_ref[bi, :, _head_cols(h)] for h in range(N_HEADS)], axis=0).astype(BF16)
        m, l, o = _sample_softmax(qb, kall_b, vall_b, cnt_b[...], bias_b[...])
        rows = lax.broadcasted_iota(jnp.int32, m.shape, 0)
        sk = jnp.zeros(m.shape, F32)
        for h in range(N_HEADS):
            sk = jnp.where(rows >> 3 == h, sink_ref[h] * LOG2E, sk)
        mm = jnp.maximum(m, sk)
        a = jnp.exp2(m - mm)
        y = o * a / (l * a + jnp.exp2(sk - mm))
        for h in range(N_HEADS):
            yb_ref[bi, :, _head_cols(h)] = y[h * SUBLANES:(h + 1) * SUBLANES]


def _sample_attention(layer, st, sinks, qa, ka_new, va_new, cache_ak, cache_av, qb, kb_new, vb_new,
                      cache_bk, cache_bv):
    depth, b, past_a, nh, e = cache_ak.shape
    past_b, n_kv = cache_bk.shape[2], cache_bk.shape[3]
    n_new = ka_new.shape[1]
    wq = nh * e
    near_rows = BAND * DILATIONS[-2]
    assert past_a == BAND * MAX_DIL and st <= DILATIONS[-2] and past_b >= BAND and st <= SUBLANES
    assert past_b % BF16_ROWS == 0 and n_new % BF16_ROWS == 0
    n_far_blocks = (past_a - near_rows) // MAX_DIL
    n_cols_a = pl.cdiv((n_far_blocks * st + near_rows + n_new) * nh, LANES) * LANES
    n_keys_b = pl.cdiv(past_b + n_new, LANES) * LANES
    assert n_far_blocks % BF16_ROWS == 0 and near_rows % BF16_ROWS == 0 and past_a % near_rows == 0
    gb = 2 if b % 2 == 0 else 1
    view = lambda c: c.reshape(depth, b, past_a // MAX_DIL, MAX_DIL, nh, e)
    far = pl.BlockSpec((None, gb, n_far_blocks, st, nh, e), lambda i: (layer, i, 0, 0, 0, 0))
    near = pl.BlockSpec((None, gb, near_rows, nh, e), lambda i: (layer, i, past_a // near_rows - 1, 0, 0))
    per_b = lambda *shape: pl.BlockSpec((gb,) + shape, lambda i: (i,) + (0,) * len(shape))
    cache_b = pl.BlockSpec((None, gb, past_b, n_kv, e), lambda i: (layer, i, 0, 0, 0))
    table = lambda n: pltpu.VMEM((N_HEADS * SUBLANES, n), F32)
    return pl.pallas_call(
        functools.partial(_sample_kernel, past_a=past_a, past_b=past_b, st=st),
        grid=(b // gb,),
        in_specs=[pl.BlockSpec(memory_space=pltpu.SMEM),
                  per_b(SUBLANES, nh, e), per_b(n_new, nh, e), per_b(n_new, nh, e), far, near, far, near,
                  per_b(SUBLANES, wq), per_b(n_new, n_kv * e), per_b(n_new, n_kv * e), cache_b, cache_b],
        out_specs=[per_b(SUBLANES, wq), per_b(SUBLANES, wq)],
        out_shape=[jax.ShapeDtypeStruct((b, SUBLANES, wq), F32), jax.ShapeDtypeStruct((b, SUBLANES, wq), F32)],
        scratch_shapes=[pltpu.VMEM((n_cols_a, e), BF16), pltpu.VMEM((n_cols_a, e), BF16),
                        pltpu.VMEM((n_kv, n_keys_b, e), BF16), pltpu.VMEM((n_kv, n_keys_b, e), BF16),
                        table(n_cols_a), table(n_keys_b), table(n_keys_b)],
        compiler_params=_params("arbitrary"),
        name="sample_attn",
    )(sinks, qa, ka_new, va_new, view(cache_ak), cache_ak, view(cache_av), cache_av,
      qb, kb_new, vb_new, cache_bk, cache_bv)


def _merge_kernel(*refs, n_groups, tn):
    if n_groups:
        o_refs, refs = refs[:n_groups], refs[n_groups:]
        lse_refs, refs = refs[:n_groups], refs[n_groups:]
    else:
        ya_ref, refs = refs[0], refs[1:]
    yb_ref, ga_ref, gb_ref, wba_ref, wbb_ref, out_ref, ya_s = refs
    if n_groups:
        s = pl.program_id(0)

        @pl.when(s == 0)
        def _():
            ya_s[1] = jnp.zeros(ya_s.shape[1:], BF16)

        ya = ya_s[(s + 1) % 2]
        lses = [r[...] for r in lse_refs]
        os_ = [_native_to_heads(r[...]) for r in o_refs]
        nxt = s % 2
        for h in range(N_HEADS):
            e = [l[:, h:h + 1] for l in lses]
            m = functools.reduce(jnp.maximum, e)
            w = [jnp.exp2(ei - m) for ei in e]
            num = sum(wi * o[h] for wi, o in zip(w, os_))
            ya_s[nxt, :, _head_cols(h)] = (num / sum(w)).astype(BF16)
    else:
        ya = ya_ref[...].astype(BF16)
    yb = yb_ref[...].astype(BF16)
    for n in range(out_ref.shape[1] // tn):
        sl = slice(n * tn, (n + 1) * tn)
        ya_d = jnp.dot(ya, wba_ref[:, sl], preferred_element_type=F32)
        yb_d = jnp.dot(yb, wbb_ref[:, sl], preferred_element_type=F32)
        out_ref[:, sl] = (ga_ref[:, sl].astype(F32) * ya_d + gb_ref[:, sl].astype(F32) * yb_d).astype(BF16)


def _outproj_kernel(mg_ref, x_ref, wo_ref, out_ref, *, tn):
    for n in range(out_ref.shape[1] // tn):
        sl = slice(n * tn, (n + 1) * tn)
        out_ref[:, sl] = x_ref[:, sl] + jnp.dot(mg_ref[...], wo_ref[:, sl], preferred_element_type=F32)


def _post(ya_parts, yb, gates, x, wba, wbb, wo, tm_merge, tm_out):
    m, d = x.shape
    wa = wba.shape[0]
    whole = lambda a: pl.BlockSpec(a.shape, lambda i: (0, 0), pipeline_mode=pl.Buffered(1))
    n_blocks = m // tm_merge
    if isinstance(ya_parts, tuple):
        os_, lses = ya_parts
        n_groups = len(os_)
        ya_args = list(os_) + list(lses)
        steps = n_blocks + 1
        ahead = lambda s: jnp.minimum(s, n_blocks - 1)
        cur = lambda s: jnp.maximum(s - 1, 0)
        ya_specs = ([pl.BlockSpec((tm_merge, N_HEADS, HEAD_DIM), lambda s: (ahead(s), 0, 0))] * n_groups
                    + [pl.BlockSpec((tm_merge, LANES), lambda s: (ahead(s), 0))] * n_groups)
        scratch = [pltpu.VMEM((2, tm_merge, wa), BF16)]
        sem = "arbitrary"
    else:
        n_groups = 0
        ya_args = [ya_parts]
        steps = n_blocks
        cur = lambda s: s
        ya_specs = [pl.BlockSpec((tm_merge, wa), lambda s: (s, 0))]
        scratch = [pltpu.VMEM((SUBLANES, LANES), BF16)]
        sem = "parallel"
    rows = lambda w, col=0: pl.BlockSpec((tm_merge, w), lambda s: (cur(s), col))
    merged = pl.pallas_call(
        functools.partial(_merge_kernel, n_groups=n_groups, tn=512),
        grid=(steps,),
        in_specs=ya_specs + [rows(wa), rows(d, 0), rows(d, 1), whole(wba), whole(wbb)],
        out_specs=rows(d),
        out_shape=jax.ShapeDtypeStruct((m, d), BF16),
        scratch_shapes=scratch,
        compiler_params=_params(sem),
        name="merge_branches",
    )(*ya_args, yb, gates, gates, wba, wbb)
    rows_o = pl.BlockSpec((tm_out, d), lambda i: (i, 0))
    return pl.pallas_call(
        functools.partial(_outproj_kernel, tn=512),
        grid=(m // tm_out,),
        in_specs=[rows_o, rows_o, whole(wo)],
        out_specs=rows_o,
        out_shape=jax.ShapeDtypeStruct((m, d), F32),
        compiler_params=_params("parallel"),
        name="out_proj",
    )(merged, x, wo)


def _ffn_kernel(x_ref, g_ref, wg_ref, wu_ref, wd_ref, out_ref, h_ref):
    f = pl.program_id(1)

    @pl.when(f == 0)
    def _():
        x = x_ref[...]
        h_ref[...] = _rms_rows(x, g_ref[...]).astype(BF16)
        out_ref[...] = x

    h = h_ref[...]
    gate = jnp.dot(h, wg_ref[...], preferred_element_type=F32)
    up = jnp.dot(h, wu_ref[...], preferred_element_type=F32)
    u = (jax.nn.silu(gate) * up).astype(BF16)
    out_ref[...] += jnp.dot(u, wd_ref[...], preferred_element_type=F32)


def _ffn(x, g, wg, wu, wd, tm, tf):
    m, d = x.shape
    dff = wg.shape[1]
    assert m % tm == 0 and dff % tf == 0
    return pl.pallas_call(
        _ffn_kernel,
        grid=(m // tm, dff // tf),
        in_specs=[pl.BlockSpec((tm, d), lambda i, f: (i, 0)), pl.BlockSpec((1, d), lambda i, f: (0, 0)),
                  pl.BlockSpec((d, tf), lambda i, f: (0, f)), pl.BlockSpec((d, tf), lambda i, f: (0, f)),
                  pl.BlockSpec((tf, d), lambda i, f: (f, 0))],
        out_specs=pl.BlockSpec((tm, d), lambda i, f: (i, 0)),
        out_shape=jax.ShapeDtypeStruct((m, d), F32),
        scratch_shapes=[pltpu.VMEM((tm, d), BF16)],
        compiler_params=_params("parallel", "arbitrary"),
        name="ffn",
    )(x, g, wg, wu, wd)


def _pad_rows(a, n):
    return jnp.pad(a, ((0, 0), (0, n - a.shape[1])) + ((0, 0),) * (a.ndim - 2))


def kernel(x_prompt, x_sample, cache_a_k, cache_a_v, cache_b_k, cache_b_v, norm_mix, w_in, qnorm_a, knorm_a,
           qnorm_b, knorm_b, sinks_b, w_branch_a, w_branch_b, w_out, norm_ffn, w_ffn_gate, w_ffn_up, w_ffn_down):
    depth = w_in.shape[0]
    b, t, d = x_prompt.shape
    sb, st, _ = x_sample.shape
    wa = N_HEADS * HEAD_DIM
    keep_b = min(BAND, t)
    xp = x_prompt.reshape(b * t, d)
    xs = x_sample.reshape(sb * st, d)
    tm_p = 512
    tm_s = sb * st
    row = lambda v: v.reshape(1, -1)
    outs = [[] for _ in range(6)]
    k_slots = v_slots = None
    for l in range(depth):
        w_in_l = _pack_w_in(w_in[l], d)
        wba, wbb, wo = w_branch_a[l].astype(BF16), w_branch_b[l].astype(BF16), w_out[l].astype(BF16)
        wg, wu, wd = w_ffn_gate[l].astype(BF16), w_ffn_up[l].astype(BF16), w_ffn_down[l].astype(BF16)
        norms = (row(qnorm_a[l]), row(knorm_a[l]), row(qnorm_b[l]), row(knorm_b[l]))

        qa, k_slots, v_slots, qb, kb, vb, gates = _inproj(xp, row(norm_mix[l]), w_in_l, *norms, tm=tm_p,
                                                          prev=(k_slots, v_slots) if l else None)
        seq = lambda a: a.reshape((b, t) + a.shape[1:])
        per_seq = lambda a: a.reshape((l + 1, b, t) + a.shape[2:])
        parts = [_band_attention_a(seq(qa), per_seq(k_slots), per_seq(v_slots), l, dil) for dil in DILATIONS]
        os_ = [o.reshape(b * t, N_HEADS, HEAD_DIM) for o, _ in parts]
        lses = [s.reshape(b * t, LANES) for _, s in parts]
        yb = _band_attention_b(seq(qb), seq(kb), seq(vb), sinks_b[l]).reshape(b * t, wa)
        x1 = _post((os_, lses), yb, gates, xp, wba, wbb, wo, tm_merge=tm_p, tm_out=2 * tm_p)
        xp = _ffn(x1, row(norm_ffn[l]), wg, wu, wd, tm=2 * tm_p, tf=512)
        outs[0].append(kb.reshape(b, t, N_KV_B, HEAD_DIM)[:, t - keep_b:])
        outs[1].append(vb.reshape(b, t, N_KV_B, HEAD_DIM)[:, t - keep_b:])

        qa, ka, va, qb, kb, vb, gates = _inproj(xs, row(norm_mix[l]), w_in_l, *norms, tm=tm_s)
        ka, va = ka[0], va[0]
        tok = lambda a: a.reshape((sb, st) + a.shape[1:])
        ya, yb = _sample_attention(
            l, st, sinks_b[l],
            _pad_rows(tok(qa), SUBLANES), _pad_rows(tok(ka), BF16_ROWS), _pad_rows(tok(va), BF16_ROWS),
            cache_a_k, cache_a_v,
            _pad_rows(tok(qb).astype(F32), SUBLANES), _pad_rows(tok(kb), BF16_ROWS), _pad_rows(tok(vb), BF16_ROWS),
            cache_b_k, cache_b_v)
        ya = ya[:, :st].reshape(sb * st, wa)
        yb = yb[:, :st].reshape(sb * st, wa)
        x1 = _post(ya, yb, gates, xs, wba, wbb, wo, tm_merge=tm_s, tm_out=tm_s)
        xs = _ffn(x1, row(norm_ffn[l]), wg, wu, wd, tm=tm_s, tf=512)
        outs[2].append(tok(ka))
        outs[3].append(tok(va))
        outs[4].append(kb.reshape(sb, st, N_KV_B, HEAD_DIM))
        outs[5].append(vb.reshape(sb, st, N_KV_B, HEAD_DIM))
    cache_a = tuple(a.reshape(depth, b, t, N_HEADS, HEAD_DIM) for a in (k_slots, v_slots))
    return (xp.reshape(b, t, d), xs.reshape(sb, st, d)) + cache_a + tuple(jnp.stack(o) for o in outs)
```

```python
import functools
import math

import jax
import jax.numpy as jnp
from jax import lax
from jax.experimental import pallas as pl
from jax.experimental.pallas import tpu as pltpu

F32 = jnp.float32
BF16 = jnp.bfloat16

HEAD_DIM = 128
N_HEADS = 8
N_KV_B = 2
BAND = 128
DILATIONS = (1, 4, 16)
MAX_DIL = DILATIONS[-1]
EPS = 1e-6
NEG = -1e30
LOG2E = math.log2(math.e)
Q_SCALE = HEAD_DIM ** -0.5 * LOG2E
LANES = 128
SUBLANES = 8
BF16_ROWS = 16
MXU_COLS = 256
VMEM_LIMIT_BYTES = 56 * 1024 * 1024

TN_IN = 1024


def _slope2(h):
    return 2.0 ** (-(h + 1)) * LOG2E


def _params(*sem):
    return pltpu.CompilerParams(dimension_semantics=sem, vmem_limit_bytes=VMEM_LIMIT_BYTES)


def _rms_rows(x, gain):
    ms = jnp.mean(x * x, axis=-1, keepdims=True)
    return x * lax.rsqrt(ms + EPS) * gain


def _head_cols(h):
    return slice(h * HEAD_DIM, (h + 1) * HEAD_DIM)


def _heads_to_native(parts):
    return jnp.swapaxes(jnp.stack(parts, axis=0), 0, 1)


def _native_to_heads(x):
    return jnp.swapaxes(x, 0, 1)


def _inproj_kernel(*refs, n_prev):
    x_ref, g_ref, w_ref, qna_ref, kna_ref, qnb_ref, knb_ref = refs[:7]
    kprev_ref, vprev_ref = refs[7:9] if n_prev else (None, None)
    qa_ref, ka_ref, va_ref, qb_ref, kb_ref, vb_ref, gates_ref, h_ref = refs[9 if n_prev else 7:]
    j = pl.program_id(1)

    @pl.when(j == 0)
    def _():
        h_ref[...] = _rms_rows(x_ref[...], g_ref[...]).astype(BF16)

    def head_pairs(first=0, n=TN_IN // MXU_COLS):
        for c in range(first, first + n):
            p = jnp.dot(h_ref[...], w_ref[:, c * MXU_COLS:(c + 1) * MXU_COLS], preferred_element_type=F32)
            yield c, p[:, :HEAD_DIM], p[:, HEAD_DIM:]

    def normed_heads(gain):
        parts = []
        for _, lo, hi in head_pairs():
            parts += [_rms_rows(lo, gain), _rms_rows(hi, gain)]
        return parts

    @pl.when(j == 0)
    def _():
        qa_ref[...] = _heads_to_native(normed_heads(qna_ref[...] * Q_SCALE))

    @pl.when(j == 1)
    def _():
        ka_ref[n_prev] = _heads_to_native(normed_heads(kna_ref[...]))
        if n_prev:
            ka_ref[0:n_prev] = kprev_ref[...]

    @pl.when(j == 2)
    def _():
        parts = []
        for _, lo, hi in head_pairs():
            parts += [lo, hi]
        va_ref[n_prev] = _heads_to_native(parts)
        if n_prev:
            va_ref[0:n_prev] = vprev_ref[...]

    @pl.when(j == 3)
    def _():
        gain = qnb_ref[...] * Q_SCALE
        for c, lo, hi in head_pairs():
            qb_ref[:, _head_cols(2 * c)] = _rms_rows(lo, gain).astype(BF16)
            qb_ref[:, _head_cols(2 * c + 1)] = _rms_rows(hi, gain).astype(BF16)

    @pl.when(j == 4)
    def _():
        for _, lo, hi in head_pairs(0, 1):
            kb_ref[:, _head_cols(0)] = _rms_rows(lo, knb_ref[...])
            kb_ref[:, _head_cols(1)] = _rms_rows(hi, knb_ref[...])
        for _, lo, hi in head_pairs(1, 1):
            vb_ref[:, _head_cols(0)] = lo
            vb_ref[:, _head_cols(1)] = hi

    @pl.when(j >= 5)
    def _():
        for c, lo, hi in head_pairs():
            gates_ref[:, _head_cols(2 * c)] = (0.5 * jnp.tanh(0.5 * lo) + 0.5).astype(BF16)
            gates_ref[:, _head_cols(2 * c + 1)] = (0.5 * jnp.tanh(0.5 * hi) + 0.5).astype(BF16)


def _pack_w_in(w, d):
    wa = N_HEADS * HEAD_DIM
    kvw = N_KV_B * HEAD_DIM
    assert w.shape[1] == 4 * wa + 2 * kvw + 2 * d and kvw == MXU_COLS and wa == TN_IN
    pad = jnp.zeros((w.shape[0], TN_IN - 2 * kvw), w.dtype)
    return jnp.concatenate([w[:, :4 * wa + 2 * kvw], pad, w[:, 4 * wa + 2 * kvw:]], axis=1).astype(BF16)


def _inproj(x, g, w, qna, kna, qnb, knb, tm, prev=None):
    m, d = x.shape
    wa = N_HEADS * HEAD_DIM
    kvw = N_KV_B * HEAD_DIM
    assert m % tm == 0 and d % TN_IN == 0
    ng = 2 * d // TN_IN
    nj = 5 + ng
    assert w.shape == (d, nj * TN_IN)
    native = pl.BlockSpec((tm, N_HEADS, HEAD_DIM), lambda i, j: (i, 0, 0))
    flat = lambda width: pl.BlockSpec((tm, width), lambda i, j: (i, 0))
    row = lambda n: pl.BlockSpec((1, n), lambda i, j: (0, 0))
    n_prev = 0 if prev is None else prev[0].shape[0]
    slots = lambda n: pl.BlockSpec((n, tm, N_HEADS, HEAD_DIM), lambda i, j: (0, i, 0, 0))
    slots_shape = jax.ShapeDtypeStruct((n_prev + 1, m, N_HEADS, HEAD_DIM), F32)
    return pl.pallas_call(
        functools.partial(_inproj_kernel, n_prev=n_prev),
        grid=(m // tm, nj),
        in_specs=[pl.BlockSpec((tm, d), lambda i, j: (i, 0)), row(d),
                  pl.BlockSpec((d, TN_IN), lambda i, j: (0, j)),
                  row(HEAD_DIM), row(HEAD_DIM), row(HEAD_DIM), row(HEAD_DIM)] + [slots(n_prev)] * (2 if n_prev else 0),
        out_specs=[native, slots(n_prev + 1), slots(n_prev + 1), flat(wa), flat(kvw), flat(kvw),
                   pl.BlockSpec((tm, TN_IN), lambda i, j: (i, jnp.clip(j - 5, 0, ng - 1)))],
        out_shape=[jax.ShapeDtypeStruct((m, N_HEADS, HEAD_DIM), F32), slots_shape, slots_shape,
                   jax.ShapeDtypeStruct((m, wa), BF16),
                   jax.ShapeDtypeStruct((m, kvw), F32), jax.ShapeDtypeStruct((m, kvw), F32),
                   jax.ShapeDtypeStruct((m, 2 * d), BF16)],
        scratch_shapes=[pltpu.VMEM((tm, d), BF16)],
        compiler_params=_params("parallel", "arbitrary"),
        name="inproj",
    )(x, g, w, qna, kna, qnb, knb, *(prev or ()))


def _band_kernel(*refs, n_kv, lc, dil, sink, native, halo, cpb=1, n_side=0):
    refs = list(refs)
    sink_ref = refs.pop(0) if sink else None
    q_ref = refs.pop(0)
    kh_ref = refs.pop(0) if halo else None
    k_ref = refs.pop(0)
    vh_ref = refs.pop(0) if halo else None
    v_ref = refs.pop(0)
    side_in = [refs.pop(0) for _ in range(n_side)]
    o_ref = refs.pop(0)
    lse_ref = None if sink else refs.pop(0)
    side_out = [refs.pop(0) for _ in range(n_side)]
    qbuf, kbuf, vbuf, bias_s = refs
    for src, dst in zip(side_in, side_out):
        dst[...] = src[...].astype(BF16)
    c = pl.program_id(2)
    off = kbuf.shape[1] - lc
    kw = BAND + off

    @pl.when((pl.program_id(0) == 0) & (pl.program_id(1) == 0) & (c == 0))
    def _():
        rows = lax.broadcasted_iota(jnp.int32, (BAND, kw), 0)
        cols = lax.broadcasted_iota(jnp.int32, (BAND, kw), 1)
        back = rows + off - cols
        in_band = (back >= 0) & (back <= BAND)
        backf = back.astype(F32)
        for h in range(N_HEADS):
            bias = -(_slope2(h) * dil) * backf
            bias_s[0, h * BAND:(h + 1) * BAND, :] = jnp.where(in_band, bias, NEG)
            bias_s[1, h * BAND:(h + 1) * BAND, :] = jnp.where(in_band & (cols >= off), bias, NEG)
        if off and not halo:
            for buf in (kbuf, vbuf):
                buf[:, 0:off, :] = jnp.zeros((buf.shape[0], off, buf.shape[2]), BF16)

    def fill(cls):
        if native:
            rows_of = (lambda ref: ref[...]) if cpb == 1 else (lambda ref: ref[:, cls])
            heads = lambda ref: _native_to_heads(rows_of(ref)).astype(BF16)
            qbuf[...] = heads(q_ref)
            for buf, halo_ref, ref in ((kbuf, kh_ref, k_ref), (vbuf, vh_ref, v_ref)):
                if halo:
                    buf[:, 0:off, :] = heads(halo_ref)
                buf[:, off:, :] = heads(ref)
        else:
            for h in range(N_HEADS):
                qbuf[h] = q_ref[:, _head_cols(h)]
            for buf, halo_ref, ref in ((kbuf, kh_ref, k_ref), (vbuf, vh_ref, v_ref)):
                for h in range(n_kv):
                    if halo:
                        buf[h, 0:off, :] = halo_ref[:, _head_cols(h)].astype(BF16)
                    buf[h, off:, :] = ref[:, _head_cols(h)].astype(BF16)

    lane = lax.broadcasted_iota(jnp.int32, (BAND, LANES), 1)
    group_shift = (N_HEADS // n_kv).bit_length() - 1
    assert 1 << group_shift == N_HEADS // n_kv

    def block(i, cls):
        r0 = pl.multiple_of(i * BAND, BAND)
        variant = jnp.where((c == 0) & (i == 0), 1, 0)
        nt = (((1,), (1,)), ((), ()))
        s = jnp.concatenate(
            [lax.dot_general(qbuf[h, pl.ds(r0, BAND), :], kbuf[h >> group_shift, pl.ds(r0, kw), :], nt,
                             preferred_element_type=F32) for h in range(N_HEADS)], axis=0)
        s = s + bias_s[variant]
        m = jnp.max(s, axis=-1, keepdims=True)
        p = jnp.exp2(s - m)
        l = jnp.sum(p, axis=-1, keepdims=True)
        lse_tile = jnp.zeros((BAND, LANES), F32)
        ys = []
        for h in range(N_HEADS):
            rs = slice(h * BAND, (h + 1) * BAND)
            o = jnp.dot(p[rs].astype(BF16), vbuf[h >> group_shift, pl.ds(r0, kw), :],
                        preferred_element_type=F32)
            mh, lh = m[rs], l[rs]
            if sink:
                sk = sink_ref[h] * LOG2E
                mm = jnp.maximum(mh, sk)
                a = jnp.exp2(mh - mm)
                ys.append(o * a / (lh * a + jnp.exp2(sk - mm)))
            else:
                ys.append(o / lh)
                lse_tile = jnp.where(lane == h, mh + jnp.log2(lh), lse_tile)
        if native and cpb == 1:
            o_ref[pl.ds(r0, BAND)] = _heads_to_native(ys)
        elif native:
            o_ref[pl.ds(r0, BAND), cls] = _heads_to_native(ys)
        else:
            for h in range(N_HEADS):
                o_ref[pl.ds(r0, BAND), _head_cols(h)] = ys[h].astype(BF16)
        if not sink:
            lse_ref[pl.ds(r0, BAND), cls * LANES:(cls + 1) * LANES] = lse_tile

    for cls in range(cpb):
        fill(cls)
        lax.fori_loop(0, lc // BAND, lambda i, carry, cls=cls: (block(i, cls), carry)[1], 0)


def _band_chunks(t, dil):
    tl = t // dil
    lc = min(tl, 4 * BAND)
    assert t % dil == 0 and tl % lc == 0 and lc % BAND == 0
    return tl, lc


def _band_scratch(n_kv, lc, off):
    return [pltpu.VMEM((N_HEADS, lc, HEAD_DIM), BF16),
            pltpu.VMEM((n_kv, off + lc, HEAD_DIM), BF16), pltpu.VMEM((n_kv, off + lc, HEAD_DIM), BF16),
            pltpu.VMEM((2, N_HEADS * BAND, off + BAND), F32)]


def _side_cast_specs(side, grid):
    steps = grid[0] * grid[1] * grid[2]
    step_of = lambda bi, r, c: (bi * grid[1] + r) * grid[2] + c
    in_specs, out_specs, out_shape = [], [], []
    for w, layer in side:
        _, rows, cols = w.shape
        assert rows % steps == 0 and (rows // steps) % BF16_ROWS == 0
        in_specs.append(pl.BlockSpec((None, rows // steps, cols),
                                     lambda bi, r, c, layer=layer: (layer, step_of(bi, r, c), 0)))
        out_specs.append(pl.BlockSpec((rows // steps, cols), lambda bi, r, c: (step_of(bi, r, c), 0)))
        out_shape.append(jax.ShapeDtypeStruct((rows, cols), BF16))
    return in_specs, out_specs, out_shape


def _band_attention_a(q, k_slots, v_slots, slot, dil, side=()):
    b, t, nh, e = q.shape
    n_slots = k_slots.shape[0]
    tl, lc = _band_chunks(t, dil)
    nc = tl // lc
    halo = nc > 1
    off = 0 if tl == BAND else BAND
    cpb = max(1, 4 * BAND // lc) if not halo else 1
    assert dil % cpb == 0
    cls_dim = None if cpb == 1 else cpb
    view = lambda a: a.reshape(b, tl, dil, nh, e)
    view_kv = lambda a: a.reshape(n_slots, b, tl, dil, nh, e)
    chunk = pl.BlockSpec((None, lc, cls_dim, nh, e), lambda bi, r, c: (bi, c, r, 0, 0))
    chunk_kv = pl.BlockSpec((None, None, lc, cls_dim, nh, e), lambda bi, r, c: (slot, bi, c, r, 0, 0))
    halo_kv = pl.BlockSpec((None, None, BAND, None, nh, e),
                           lambda bi, r, c: (slot, bi, jnp.maximum(c * (lc // BAND) - 1, 0), r, 0, 0))
    kv_specs = [halo_kv, chunk_kv] if halo else [chunk_kv]
    kv_args = lambda a: [view_kv(a)] * len(kv_specs)
    grid = (b, dil // cpb, nc)
    side_in, side_out, side_shape = _side_cast_specs(side, grid)
    o, lse, *cast = pl.pallas_call(
        functools.partial(_band_kernel, n_kv=nh, lc=lc, dil=dil, sink=False, native=True, halo=halo, cpb=cpb,
                          n_side=len(side)),
        grid=grid,
        in_specs=[chunk] + kv_specs + kv_specs + side_in,
        out_specs=[chunk, pl.BlockSpec((None, lc, cpb * LANES), lambda bi, r, c: (bi, c, r))] + side_out,
        out_shape=[jax.ShapeDtypeStruct((b, tl, dil, nh, e), F32),
                   jax.ShapeDtypeStruct((b, tl, dil * LANES), F32)] + side_shape,
        scratch_shapes=_band_scratch(nh, lc, off),
        compiler_params=_params("arbitrary", "arbitrary", "arbitrary"),
        name=f"band_d{dil}",
    )(view(q), *kv_args(k_slots), *kv_args(v_slots), *[w for w, _ in side])
    return o.reshape(b, t, nh, e), lse.reshape(b, t, LANES), cast


def _band_attention_b(q, k, v, sinks, side=()):
    b, t, wq = q.shape
    wk = k.shape[2]
    tl, lc = _band_chunks(t, 1)
    nc = tl // lc
    halo = nc > 1
    off = 0 if tl == BAND else BAND
    chunk = lambda w: pl.BlockSpec((None, lc, w), lambda bi, r, c: (bi, c, 0))
    halo_kv = pl.BlockSpec((None, BAND, wk), lambda bi, r, c: (bi, jnp.maximum(c * (lc // BAND) - 1, 0), 0))
    kv_specs = [halo_kv, chunk(wk)] if halo else [chunk(wk)]
    n_kv = wk // HEAD_DIM
    grid = (b, 1, nc)
    side_in, side_out, side_shape = _side_cast_specs(side, grid)
    o, *cast = pl.pallas_call(
        functools.partial(_band_kernel, n_kv=n_kv, lc=lc, dil=1, sink=True, native=False, halo=halo,
                          n_side=len(side)),
        grid=grid,
        in_specs=[pl.BlockSpec(memory_space=pltpu.SMEM), chunk(wq)] + kv_specs + kv_specs + side_in,
        out_specs=[chunk(wq)] + side_out,
        out_shape=[jax.ShapeDtypeStruct((b, t, wq), BF16)] + side_shape,
        scratch_shapes=_band_scratch(n_kv, lc, off),
        compiler_params=_params("arbitrary", "arbitrary", "arbitrary"),
        name="band_sink",
    )(sinks, q, *([k] * len(kv_specs)), *([v] * len(kv_specs)), *[w for w, _ in side])
    return o, cast


def _sample_key_table(past, st, n_far, n_near, n_cols, dilations, heads_per_key=1):
    shape = (N_HEADS * SUBLANES, n_cols)
    rows = lax.broadcasted_iota(jnp.int32, shape, 0)
    col = lax.broadcasted_iota(jnp.int32, shape, 1)
    j = col // heads_per_key
    far = (j // st) * MAX_DIL + j % st
    near = past - n_near + (j - n_far)
    n = jnp.where(j < n_far, far, near)
    dist = past + (rows & (SUBLANES - 1)) - n
    cnt = jnp.zeros(shape, F32)
    for dil in dilations:
        ok = (dist >= 0) & (dist <= BAND * dil) & ((dist & (dil - 1)) == 0)
        cnt = cnt + ok.astype(F32)
    if heads_per_key > 1:
        cnt = jnp.where(col % heads_per_key == rows >> 3, cnt, 0.0)
    slope2 = jnp.exp2(-((rows >> 3) + 1).astype(F32)) * LOG2E
    return cnt, jnp.where(cnt > 0.0, -slope2 * dist.astype(F32), NEG)


def _sample_softmax(q, kall, vall, cnt, bias):
    nt = (((1,), (1,)), ((), ()))
    group = N_HEADS // kall.shape[0]
    s = jnp.concatenate([lax.dot_general(q[h], kall[h // group], nt, preferred_element_type=F32)
                         for h in range(N_HEADS)], axis=0)
    s = s + bias
    m = jnp.max(s, axis=-1, keepdims=True)
    p = cnt * jnp.exp2(s - m)
    l = jnp.sum(p, axis=-1, keepdims=True)
    o = jnp.concatenate([jnp.dot(p[h * SUBLANES:(h + 1) * SUBLANES].astype(BF16), vall[h // group],
                                 preferred_element_type=F32) for h in range(N_HEADS)], axis=0)
    return m, l, o


def _sample_kernel(sink_ref, qa_ref, kan_ref, van_ref, kfar_ref, knear_ref, vfar_ref, vnear_ref,
                   qb_ref, kbn_ref, vbn_ref, cbk_ref, cbv_ref, ya_ref, yb_ref,
                   kall_a, vall_a, kall_b, vall_b, bias_a, cnt_b, bias_b, *, past_a, past_b, st):
    n_far = kfar_ref.shape[1] * kfar_ref.shape[2]
    n_near = knear_ref.shape[1]
    n_new = kan_ref.shape[1]
    used_a = n_far + n_near + n_new
    used_b = past_b + n_new

    @pl.when(pl.program_id(0) == 0)
    def _():
        for buf, used in ((kall_a, used_a * N_HEADS), (vall_a, used_a * N_HEADS)):
            if buf.shape[0] > used:
                buf[used:, :] = jnp.zeros((buf.shape[0] - used, buf.shape[1]), BF16)
        for buf in (kall_b, vall_b):
            buf[:, used_b:, :] = jnp.zeros((buf.shape[0], buf.shape[1] - used_b, buf.shape[2]), BF16)
        ca, ba = _sample_key_table(past_a, st, n_far, n_near, kall_a.shape[0], DILATIONS, N_HEADS)
        bias_a[...] = ba + jnp.log2(jnp.maximum(ca, 1.0))
        cb, bb = _sample_key_table(past_b, st, 0, past_b, kall_b.shape[1], (1,))
        cnt_b[...] = cb
        bias_b[...] = bb

    rows2d = lambda x: x.reshape(-1, HEAD_DIM).astype(BF16)
    for bi in range(qa_ref.shape[0]):
        for src_far, src_near, src_new, dst in ((kfar_ref, knear_ref, kan_ref, kall_a),
                                                (vfar_ref, vnear_ref, van_ref, vall_a)):
            dst[0:n_far * N_HEADS, :] = rows2d(src_far[bi])
            dst[n_far * N_HEADS:(n_far + n_near) * N_HEADS, :] = rows2d(src_near[bi])
            dst[(n_far + n_near) * N_HEADS:used_a * N_HEADS, :] = rows2d(src_new[bi])
        for src_old, src_new, dst in ((cbk_ref, kbn_ref, kall_b), (cbv_ref, vbn_ref, vall_b)):
            for h in range(N_KV_B):
                dst[h, 0:past_b, :] = src_old[bi, :, h, :].astype(BF16)
                dst[h, past_b:used_b, :] = src_new[bi, :, _head_cols(h)].astype(BF16)

        qa = _native_to_heads(qa_ref[bi]).reshape(N_HEADS * SUBLANES, HEAD_DIM).astype(BF16)
        s = lax.dot_general(qa, kall_a[...], (((1,), (1,)), ((), ())), preferred_element_type=F32) + bias_a[...]
        m = jnp.max(s, axis=-1, keepdims=True)
        p = jnp.exp2(s - m)
        l = jnp.sum(p, axis=-1, keepdims=True)
        y = jnp.dot(p.astype(BF16), vall_a[...], preferred_element_type=F32) / l
        for h in range(N_HEADS):
            ya_ref[bi, :, _head_cols(h)] = y[h * SUBLANES:(h + 1) * SUBLANES]

        qb = jnp.stack([qb_ref[bi, :, _head_cols(h)] for h in range(N_HEADS)], axis=0).astype(BF16)
        m, l, o = _sample_softmax(qb, kall_b, vall_b, cnt_b[...], bias_b[...])
        rows = lax.broadcasted_iota(jnp.int32, m.shape, 0)
        sk = jnp.zeros(m.shape, F32)
        for h in range(N_HEADS):
            sk = jnp.where(rows >> 3 == h, sink_ref[h] * LOG2E, sk)
        mm = jnp.maximum(m, sk)
        a = jnp.exp2(m - mm)
        y = o * a / (l * a + jnp.exp2(sk - mm))
        for h in range(N_HEADS):
            yb_ref[bi, :, _head_cols(h)] = y[h * SUBLANES:(h + 1) * SUBLANES]


def _sample_attention(layer, st, sinks, qa, ka_new, va_new, cache_ak, cache_av, qb, kb_new, vb_new,
                      cache_bk, cache_bv):
    depth, b, past_a, nh, e = cache_ak.shape
    past_b, n_kv = cache_bk.shape[2], cache_bk.shape[3]
    n_new = ka_new.shape[1]
    wq = nh * e
    near_rows = BAND * DILATIONS[-2]
    assert past_a == BAND * MAX_DIL and st <= DILATIONS[-2] and past_b >= BAND and st <= SUBLANES
    assert past_b % BF16_ROWS == 0 and n_new % BF16_ROWS == 0
    n_far_blocks = (past_a - near_rows) // MAX_DIL
    n_cols_a = pl.cdiv((n_far_blocks * st + near_rows + n_new) * nh, LANES) * LANES
    n_keys_b = pl.cdiv(past_b + n_new, LANES) * LANES
    assert n_far_blocks % BF16_ROWS == 0 and near_rows % BF16_ROWS == 0 and past_a % near_rows == 0
    gb = 2 if b % 2 == 0 else 1
    view = lambda c: c.reshape(depth, b, past_a // MAX_DIL, MAX_DIL, nh, e)
    far = pl.BlockSpec((None, gb, n_far_blocks, st, nh, e), lambda i: (layer, i, 0, 0, 0, 0))
    near = pl.BlockSpec((None, gb, near_rows, nh, e), lambda i: (layer, i, past_a // near_rows - 1, 0, 0))
    per_b = lambda *shape: pl.BlockSpec((gb,) + shape, lambda i: (i,) + (0,) * len(shape))
    cache_b = pl.BlockSpec((None, gb, past_b, n_kv, e), lambda i: (layer, i, 0, 0, 0))
    table = lambda n: pltpu.VMEM((N_HEADS * SUBLANES, n), F32)
    return pl.pallas_call(
        functools.partial(_sample_kernel, past_a=past_a, past_b=past_b, st=st),
        grid=(b // gb,),
        in_specs=[pl.BlockSpec(memory_space=pltpu.SMEM),
                  per_b(SUBLANES, nh, e), per_b(n_new, nh, e), per_b(n_new, nh, e), far, near, far, near,
                  per_b(SUBLANES, wq), per_b(n_new, n_kv * e), per_b(n_new, n_kv * e), cache_b, cache_b],
        out_specs=[per_b(SUBLANES, wq), per_b(SUBLANES, wq)],
        out_shape=[jax.ShapeDtypeStruct((b, SUBLANES, wq), F32), jax.ShapeDtypeStruct((b, SUBLANES, wq), F32)],
        scratch_shapes=[pltpu.VMEM((n_cols_a, e), BF16), pltpu.VMEM((n_cols_a, e), BF16),
                        pltpu.VMEM((n_kv, n_keys_b, e), BF16), pltpu.VMEM((n_kv, n_keys_b, e), BF16),
                        table(n_cols_a), table(n_keys_b), table(n_keys_b)],
        compiler_params=_params("arbitrary"),
        name="sample_attn",
    )(sinks, qa, ka_new, va_new, view(cache_ak), cache_ak, view(cache_av), cache_av,
      qb, kb_new, vb_new, cache_bk, cache_bv)


def _merge_kernel(*refs, n_groups, tn):
    if n_groups:
        o_refs, refs = refs[:n_groups], refs[n_groups:]
        lse_refs, refs = refs[:n_groups], refs[n_groups:]
    else:
        ya_ref, refs = refs[0], refs[1:]
    yb_ref, ga_ref, gb_ref, wba_ref, wbb_ref, out_ref, ya_s = refs
    if n_groups:
        s = pl.program_id(0)

        @pl.when(s == 0)
        def _():
            ya_s[1] = jnp.zeros(ya_s.shape[1:], BF16)

        ya = ya_s[(s + 1) % 2]
        lses = [r[...] for r in lse_refs]
        os_ = [_native_to_heads(r[...]) for r in o_refs]
        nxt = s % 2
        for h in range(N_HEADS):
            e = [l[:, h:h + 1] for l in lses]
            m = functools.reduce(jnp.maximum, e)
            w = [jnp.exp2(ei - m) for ei in e]
            num = sum(wi * o[h] for wi, o in zip(w, os_))
            ya_s[nxt, :, _head_cols(h)] = (num / sum(w)).astype(BF16)
    else:
        ya = ya_ref[...].astype(BF16)
    yb = yb_ref[...].astype(BF16)
    for n in range(out_ref.shape[1] // tn):
        sl = slice(n * tn, (n + 1) * tn)
        ya_d = jnp.dot(ya, wba_ref[:, sl], preferred_element_type=F32)
        yb_d = jnp.dot(yb, wbb_ref[:, sl], preferred_element_type=F32)
        out_ref[:, sl] = (ga_ref[:, sl].astype(F32) * ya_d + gb_ref[:, sl].astype(F32) * yb_d).astype(BF16)


def _outproj_kernel(mg_ref, x_ref, wo_ref, out_ref, *, tn):
    for n in range(out_ref.shape[1] // tn):
        sl = slice(n * tn, (n + 1) * tn)
        out_ref[:, sl] = x_ref[:, sl] + jnp.dot(mg_ref[...], wo_ref[:, sl], preferred_element_type=F32)


def _post(ya_parts, yb, gates, x, wba, wbb, wo, tm_merge, tm_out):
    m, d = x.shape
    wa = wba.shape[0]
    whole = lambda a: pl.BlockSpec(a.shape, lambda i: (0, 0), pipeline_mode=pl.Buffered(1))
    n_blocks = m // tm_merge
    if isinstance(ya_parts, tuple):
        os_, lses = ya_parts
        n_groups = len(os_)
        ya_args = list(os_) + list(lses)
        steps = n_blocks + 1
        ahead = lambda s: jnp.minimum(s, n_blocks - 1)
        cur = lambda s: jnp.maximum(s - 1, 0)
        ya_specs = ([pl.BlockSpec((tm_merge, N_HEADS, HEAD_DIM), lambda s: (ahead(s), 0, 0))] * n_groups
                    + [pl.BlockSpec((tm_merge, LANES), lambda s: (ahead(s), 0))] * n_groups)
        scratch = [pltpu.VMEM((2, tm_merge, wa), BF16)]
        sem = "arbitrary"
    else:
        n_groups = 0
        ya_args = [ya_parts]
        steps = n_blocks
        cur = lambda s: s
        ya_specs = [pl.BlockSpec((tm_merge, wa), lambda s: (s, 0))]
        scratch = [pltpu.VMEM((SUBLANES, LANES), BF16)]
        sem = "parallel"
    rows = lambda w, col=0: pl.BlockSpec((tm_merge, w), lambda s: (cur(s), col))
    merged = pl.pallas_call(
        functools.partial(_merge_kernel, n_groups=n_groups, tn=512),
        grid=(steps,),
        in_specs=ya_specs + [rows(wa), rows(d, 0), rows(d, 1), whole(wba), whole(wbb)],
        out_specs=rows(d),
        out_shape=jax.ShapeDtypeStruct((m, d), BF16),
        scratch_shapes=scratch,
        compiler_params=_params(sem),
        name="merge_branches",
    )(*ya_args, yb, gates, gates, wba, wbb)
    rows_o = pl.BlockSpec((tm_out, d), lambda i: (i, 0))
    return pl.pallas_call(
        functools.partial(_outproj_kernel, tn=512),
        grid=(m // tm_out,),
        in_specs=[rows_o, rows_o, whole(wo)],
        out_specs=rows_o,
        out_shape=jax.ShapeDtypeStruct((m, d), F32),
        compiler_params=_params("parallel"),
        name="out_proj",
    )(merged, x, wo)


def _ffn_kernel(x_ref, g_ref, wg_ref, wu_ref, wd_ref, out_ref, h_ref):
    f = pl.program_id(1)

    @pl.when(f == 0)
    def _():
        x = x_ref[...]
        h_ref[...] = _rms_rows(x, g_ref[...]).astype(BF16)
        out_ref[...] = x

    h = h_ref[...]
    gate = jnp.dot(h, wg_ref[...], preferred_element_type=F32)
    up = jnp.dot(h, wu_ref[...], preferred_element_type=F32)
    u = (jax.nn.silu(gate) * up).astype(BF16)
    out_ref[...] += jnp.dot(u, wd_ref[...], preferred_element_type=F32)


def _ffn(x, g, wg, wu, wd, tm, tf):
    m, d = x.shape
    dff = wg.shape[1]
    assert m % tm == 0 and dff % tf == 0
    return pl.pallas_call(
        _ffn_kernel,
        grid=(m // tm, dff // tf),
        in_specs=[pl.BlockSpec((tm, d), lambda i, f: (i, 0)), pl.BlockSpec((1, d), lambda i, f: (0, 0)),
                  pl.BlockSpec((d, tf), lambda i, f: (0, f)), pl.BlockSpec((d, tf), lambda i, f: (0, f)),
                  pl.BlockSpec((tf, d), lambda i, f: (f, 0))],
        out_specs=pl.BlockSpec((tm, d), lambda i, f: (i, 0)),
        out_shape=jax.ShapeDtypeStruct((m, d), F32),
        scratch_shapes=[pltpu.VMEM((tm, d), BF16)],
        compiler_params=_params("parallel", "arbitrary"),
        name="ffn",
    )(x, g, wg, wu, wd)


def _pad_rows(a, n):
    return jnp.pad(a, ((0, 0), (0, n - a.shape[1])) + ((0, 0),) * (a.ndim - 2))


def kernel(x_prompt, x_sample, cache_a_k, cache_a_v, cache_b_k, cache_b_v, norm_mix, w_in, qnorm_a, knorm_a,
           qnorm_b, knorm_b, sinks_b, w_branch_a, w_branch_b, w_out, norm_ffn, w_ffn_gate, w_ffn_up, w_ffn_down):
    depth = w_in.shape[0]
    b, t, d = x_prompt.shape
    sb, st, _ = x_sample.shape
    wa = N_HEADS * HEAD_DIM
    keep_b = min(BAND, t)
    xp = x_prompt.reshape(b * t, d)
    xs = x_sample.reshape(sb * st, d)
    tm_p = 512
    tm_s = sb * st
    row = lambda v: v.reshape(1, -1)
    outs = [[] for _ in range(6)]
    k_slots = v_slots = None
    for l in range(depth):
        w_in_l = _pack_w_in(w_in[l], d)
        norms = (row(qnorm_a[l]), row(knorm_a[l]), row(qnorm_b[l]), row(knorm_b[l]))

        qa, k_slots, v_slots, qb, kb, vb, gates = _inproj(xp, row(norm_mix[l]), w_in_l, *norms, tm=tm_p,
                                                          prev=(k_slots, v_slots) if l else None)
        seq = lambda a: a.reshape((b, t) + a.shape[1:])
        per_seq = lambda a: a.reshape((l + 1, b, t) + a.shape[2:])
        ffn_w = (w_ffn_gate, w_ffn_up, w_ffn_down)
        assert len(DILATIONS) == len(ffn_w)
        parts = [_band_attention_a(seq(qa), per_seq(k_slots), per_seq(v_slots), l, dil, side=[(w, l)])
                 for dil, w in zip(DILATIONS, ffn_w)]
        os_ = [o.reshape(b * t, N_HEADS, HEAD_DIM) for o, _, _ in parts]
        lses = [s.reshape(b * t, LANES) for _, s, _ in parts]
        (wg,), (wu,), (wd,) = [cast for _, _, cast in parts]
        yb, (wo, wba, wbb) = _band_attention_b(seq(qb), seq(kb), seq(vb), sinks_b[l],
                                               side=[(w_out, l), (w_branch_a, l), (w_branch_b, l)])
        yb = yb.reshape(b * t, wa)
        x1 = _post((os_, lses), yb, gates, xp, wba, wbb, wo, tm_merge=tm_p, tm_out=2 * tm_p)
        xp = _ffn(x1, row(norm_ffn[l]), wg, wu, wd, tm=2 * tm_p, tf=512)
        outs[0].append(kb.reshape(b, t, N_KV_B, HEAD_DIM)[:, t - keep_b:])
        outs[1].append(vb.reshape(b, t, N_KV_B, HEAD_DIM)[:, t - keep_b:])

        qa, ka, va, qb, kb, vb, gates = _inproj(xs, row(norm_mix[l]), w_in_l, *norms, tm=tm_s)
        ka, va = ka[0], va[0]
        tok = lambda a: a.reshape((sb, st) + a.shape[1:])
        ya, yb = _sample_attention(
            l, st, sinks_b[l],
            _pad_rows(tok(qa), SUBLANES), _pad_rows(tok(ka), BF16_ROWS), _pad_rows(tok(va), BF16_ROWS),
            cache_a_k, cache_a_v,
            _pad_rows(tok(qb).astype(F32), SUBLANES), _pad_rows(tok(kb), BF16_ROWS), _pad_rows(tok(vb), BF16_ROWS),
            cache_b_k, cache_b_v)
        ya = ya[:, :st].reshape(sb * st, wa)
        yb = yb[:, :st].reshape(sb * st, wa)
        x1 = _post(ya, yb, gates, xs, wba, wbb, wo, tm_merge=tm_s, tm_out=tm_s)
        xs = _ffn(x1, row(norm_ffn[l]), wg, wu, wd, tm=tm_s, tf=512)
        outs[2].append(tok(ka))
        outs[3].append(tok(va))
        outs[4].append(kb.reshape(sb, st, N_KV_B, HEAD_DIM))
        outs[5].append(vb.reshape(sb, st, N_KV_B, HEAD_DIM))
    cache_a = tuple(a.reshape(depth, b, t, N_HEADS, HEAD_DIM) for a in (k_slots, v_slots))
    return (xp.reshape(b, t, d), xs.reshape(sb, st, d)) + cache_a + tuple(jnp.stack(o) for o in outs)
```

```python
import functools
import math

import jax
import jax.numpy as jnp
from jax import lax
from jax.experimental import pallas as pl
from jax.experimental.pallas import tpu as pltpu

F32 = jnp.float32
BF16 = jnp.bfloat16

HEAD_DIM = 128
N_HEADS = 8
N_KV_B = 2
BAND = 128
DILATIONS = (1, 4, 16)
MAX_DIL = DILATIONS[-1]
EPS = 1e-6
NEG = -1e30
LOG2E = math.log2(math.e)
Q_SCALE = HEAD_DIM ** -0.5 * LOG2E
LANES = 128
SUBLANES = 8
BF16_ROWS = 16
MXU_COLS = 256
VMEM_LIMIT_BYTES = 56 * 1024 * 1024

TN_IN = 1024


def _slope2(h):
    return 2.0 ** (-(h + 1)) * LOG2E


def _params(*sem):
    return pltpu.CompilerParams(dimension_semantics=sem, vmem_limit_bytes=VMEM_LIMIT_BYTES)


def _rms_rows(x, gain):
    ms = jnp.mean(x * x, axis=-1, keepdims=True)
    return x * lax.rsqrt(ms + EPS) * gain


def _head_cols(h):
    return slice(h * HEAD_DIM, (h + 1) * HEAD_DIM)


def _heads_to_native(parts):
    return jnp.swapaxes(jnp.stack(parts, axis=0), 0, 1)


def _native_to_heads(x):
    return jnp.swapaxes(x, 0, 1)


def _inproj_kernel(*refs, n_prev):
    x_ref, g_ref, w_ref, qna_ref, kna_ref, qnb_ref, knb_ref = refs[:7]
    kprev_ref, vprev_ref = refs[7:9] if n_prev else (None, None)
    qa_ref, ka_ref, va_ref, qb_ref, kb_ref, vb_ref, gates_ref, h_ref = refs[9 if n_prev else 7:]
    j = pl.program_id(1)

    @pl.when(j == 0)
    def _():
        h_ref[...] = _rms_rows(x_ref[...], g_ref[...]).astype(BF16)

    def head_pairs(first=0, n=TN_IN // MXU_COLS):
        for c in range(first, first + n):
            p = jnp.dot(h_ref[...], w_ref[:, c * MXU_COLS:(c + 1) * MXU_COLS], preferred_element_type=F32)
            yield c, p[:, :HEAD_DIM], p[:, HEAD_DIM:]

    def normed_heads(gain):
        parts = []
        for _, lo, hi in head_pairs():
            parts += [_rms_rows(lo, gain), _rms_rows(hi, gain)]
        return parts

    @pl.when(j == 0)
    def _():
        qa_ref[...] = _heads_to_native(normed_heads(qna_ref[...] * Q_SCALE))

    @pl.when(j == 1)
    def _():
        ka_ref[n_prev] = _heads_to_native(normed_heads(kna_ref[...]))
        if n_prev:
            ka_ref[0:n_prev] = kprev_ref[...]

    @pl.when(j == 2)
    def _():
        parts = []
        for _, lo, hi in head_pairs():
            parts += [lo, hi]
        va_ref[n_prev] = _heads_to_native(parts)
        if n_prev:
            va_ref[0:n_prev] = vprev_ref[...]

    @pl.when(j == 3)
    def _():
        gain = qnb_ref[...] * Q_SCALE
        for c, lo, hi in head_pairs():
            qb_ref[:, _head_cols(2 * c)] = _rms_rows(lo, gain).astype(BF16)
            qb_ref[:, _head_cols(2 * c + 1)] = _rms_rows(hi, gain).astype(BF16)

    @pl.when(j == 4)
    def _():
        for _, lo, hi in head_pairs(0, 1):
            kb_ref[:, _head_cols(0)] = _rms_rows(lo, knb_ref[...])
            kb_ref[:, _head_cols(1)] = _rms_rows(hi, knb_ref[...])
        for _, lo, hi in head_pairs(1, 1):
            vb_ref[:, _head_cols(0)] = lo
            vb_ref[:, _head_cols(1)] = hi

    @pl.when(j >= 5)
    def _():
        for c, lo, hi in head_pairs():
            gates_ref[:, _head_cols(2 * c)] = (0.5 * jnp.tanh(0.5 * lo) + 0.5).astype(BF16)
            gates_ref[:, _head_cols(2 * c + 1)] = (0.5 * jnp.tanh(0.5 * hi) + 0.5).astype(BF16)


def _pack_w_in(w, d):
    wa = N_HEADS * HEAD_DIM
    kvw = N_KV_B * HEAD_DIM
    assert w.shape[1] == 4 * wa + 2 * kvw + 2 * d and kvw == MXU_COLS and wa == TN_IN
    pad = jnp.zeros((w.shape[0], TN_IN - 2 * kvw), w.dtype)
    return jnp.concatenate([w[:, :4 * wa + 2 * kvw], pad, w[:, 4 * wa + 2 * kvw:]], axis=1).astype(BF16)


def _inproj(x, g, w, qna, kna, qnb, knb, tm, prev=None):
    m, d = x.shape
    wa = N_HEADS * HEAD_DIM
    kvw = N_KV_B * HEAD_DIM
    assert m % tm == 0 and d % TN_IN == 0
    ng = 2 * d // TN_IN
    nj = 5 + ng
    assert w.shape == (d, nj * TN_IN)
    native = pl.BlockSpec((tm, N_HEADS, HEAD_DIM), lambda i, j: (i, 0, 0))
    flat = lambda width: pl.BlockSpec((tm, width), lambda i, j: (i, 0))
    row = lambda n: pl.BlockSpec((1, n), lambda i, j: (0, 0))
    n_prev = 0 if prev is None else prev[0].shape[0]
    slots = lambda n: pl.BlockSpec((n, tm, N_HEADS, HEAD_DIM), lambda i, j: (0, i, 0, 0))
    slots_shape = jax.ShapeDtypeStruct((n_prev + 1, m, N_HEADS, HEAD_DIM), F32)
    return pl.pallas_call(
        functools.partial(_inproj_kernel, n_prev=n_prev),
        grid=(m // tm, nj),
        in_specs=[pl.BlockSpec((tm, d), lambda i, j: (i, 0)), row(d),
                  pl.BlockSpec((d, TN_IN), lambda i, j: (0, j)),
                  row(HEAD_DIM), row(HEAD_DIM), row(HEAD_DIM), row(HEAD_DIM)] + [slots(n_prev)] * (2 if n_prev else 0),
        out_specs=[native, slots(n_prev + 1), slots(n_prev + 1), flat(wa), flat(kvw), flat(kvw),
                   pl.BlockSpec((tm, TN_IN), lambda i, j: (i, jnp.clip(j - 5, 0, ng - 1)))],
        out_shape=[jax.ShapeDtypeStruct((m, N_HEADS, HEAD_DIM), F32), slots_shape, slots_shape,
                   jax.ShapeDtypeStruct((m, wa), BF16),
                   jax.ShapeDtypeStruct((m, kvw), F32), jax.ShapeDtypeStruct((m, kvw), F32),
                   jax.ShapeDtypeStruct((m, 2 * d), BF16)],
        scratch_shapes=[pltpu.VMEM((tm, d), BF16)],
        compiler_params=_params("parallel", "arbitrary"),
        name="inproj",
    )(x, g, w, qna, kna, qnb, knb, *(prev or ()))


def _band_kernel(*refs, n_kv, lc, dil, sink, native, halo, cpb=1, n_side=0):
    refs = list(refs)
    sink_ref = refs.pop(0) if sink else None
    q_ref = refs.pop(0)
    kh_ref = refs.pop(0) if halo else None
    k_ref = refs.pop(0)
    vh_ref = refs.pop(0) if halo else None
    v_ref = refs.pop(0)
    side_in = [refs.pop(0) for _ in range(n_side)]
    o_ref = refs.pop(0)
    lse_ref = None if sink else refs.pop(0)
    side_out = [refs.pop(0) for _ in range(n_side)]
    qbuf, kbuf, vbuf, bias_s = refs
    for src, dst in zip(side_in, side_out):
        dst[...] = src[...].astype(BF16)
    c = pl.program_id(2)
    off = kbuf.shape[1] - lc
    kw = BAND + off

    @pl.when((pl.program_id(0) == 0) & (pl.program_id(1) == 0) & (c == 0))
    def _():
        rows = lax.broadcasted_iota(jnp.int32, (BAND, kw), 0)
        cols = lax.broadcasted_iota(jnp.int32, (BAND, kw), 1)
        back = rows + off - cols
        in_band = (back >= 0) & (back <= BAND)
        backf = back.astype(F32)
        for h in range(N_HEADS):
            bias = -(_slope2(h) * dil) * backf
            bias_s[0, h * BAND:(h + 1) * BAND, :] = jnp.where(in_band, bias, NEG)
            bias_s[1, h * BAND:(h + 1) * BAND, :] = jnp.where(in_band & (cols >= off), bias, NEG)
        if off and not halo:
            for buf in (kbuf, vbuf):
                buf[:, 0:off, :] = jnp.zeros((buf.shape[0], off, buf.shape[2]), BF16)

    def fill(cls):
        if native:
            rows_of = (lambda ref: ref[...]) if cpb == 1 else (lambda ref: ref[:, cls])
            heads = lambda ref: _native_to_heads(rows_of(ref)).astype(BF16)
            qbuf[...] = heads(q_ref)
            for buf, halo_ref, ref in ((kbuf, kh_ref, k_ref), (vbuf, vh_ref, v_ref)):
                if halo:
                    buf[:, 0:off, :] = heads(halo_ref)
                buf[:, off:, :] = heads(ref)
        else:
            for h in range(N_HEADS):
                qbuf[h] = q_ref[:, _head_cols(h)]
            for buf, halo_ref, ref in ((kbuf, kh_ref, k_ref), (vbuf, vh_ref, v_ref)):
                for h in range(n_kv):
                    if halo:
                        buf[h, 0:off, :] = halo_ref[:, _head_cols(h)].astype(BF16)
                    buf[h, off:, :] = ref[:, _head_cols(h)].astype(BF16)

    lane = lax.broadcasted_iota(jnp.int32, (BAND, LANES), 1)
    group_shift = (N_HEADS // n_kv).bit_length() - 1
    assert 1 << group_shift == N_HEADS // n_kv

    def block(i, cls):
        r0 = pl.multiple_of(i * BAND, BAND)
        variant = jnp.where((c == 0) & (i == 0), 1, 0)
        nt = (((1,), (1,)), ((), ()))
        s = jnp.concatenate(
            [lax.dot_general(qbuf[h, pl.ds(r0, BAND), :], kbuf[h >> group_shift, pl.ds(r0, kw), :], nt,
                             preferred_element_type=F32) for h in range(N_HEADS)], axis=0)
        s = s + bias_s[variant]
        m = jnp.max(s, axis=-1, keepdims=True)
        p = jnp.exp2(s - m)
        l = jnp.sum(p, axis=-1, keepdims=True)
        lse_tile = jnp.zeros((BAND, LANES), F32)
        ys = []
        for h in range(N_HEADS):
            rs = slice(h * BAND, (h + 1) * BAND)
            o = jnp.dot(p[rs].astype(BF16), vbuf[h >> group_shift, pl.ds(r0, kw), :],
                        preferred_element_type=F32)
            mh, lh = m[rs], l[rs]
            if sink:
                sk = sink_ref[h] * LOG2E
                mm = jnp.maximum(mh, sk)
                a = jnp.exp2(mh - mm)
                ys.append(o * a / (lh * a + jnp.exp2(sk - mm)))
            else:
                ys.append(o / lh)
                lse_tile = jnp.where(lane == h, mh + jnp.log2(lh), lse_tile)
        if native and cpb == 1:
            o_ref[pl.ds(r0, BAND)] = _heads_to_native(ys)
        elif native:
            o_ref[pl.ds(r0, BAND), cls] = _heads_to_native(ys)
        else:
            for h in range(N_HEADS):
                o_ref[pl.ds(r0, BAND), _head_cols(h)] = ys[h].astype(BF16)
        if not sink:
            lse_ref[pl.ds(r0, BAND), cls * LANES:(cls + 1) * LANES] = lse_tile

    for cls in range(cpb):
        fill(cls)
        lax.fori_loop(0, lc // BAND, lambda i, carry, cls=cls: (block(i, cls), carry)[1], 0)


def _band_chunks(t, dil):
    tl = t // dil
    lc = min(tl, 4 * BAND)
    assert t % dil == 0 and tl % lc == 0 and lc % BAND == 0
    return tl, lc


def _band_scratch(n_kv, lc, off):
    return [pltpu.VMEM((N_HEADS, lc, HEAD_DIM), BF16),
            pltpu.VMEM((n_kv, off + lc, HEAD_DIM), BF16), pltpu.VMEM((n_kv, off + lc, HEAD_DIM), BF16),
            pltpu.VMEM((2, N_HEADS * BAND, off + BAND), F32)]


def _side_cast_specs(side, grid):
    steps = grid[0] * grid[1] * grid[2]
    step_of = lambda bi, r, c: (bi * grid[1] + r) * grid[2] + c
    in_specs, out_specs, out_shape = [], [], []
    for w, layer in side:
        _, rows, cols = w.shape
        assert rows % steps == 0 and (rows // steps) % BF16_ROWS == 0
        in_specs.append(pl.BlockSpec((None, rows // steps, cols),
                                     lambda bi, r, c, layer=layer: (layer, step_of(bi, r, c), 0)))
        out_specs.append(pl.BlockSpec((rows // steps, cols), lambda bi, r, c: (step_of(bi, r, c), 0)))
        out_shape.append(jax.ShapeDtypeStruct((rows, cols), BF16))
    return in_specs, out_specs, out_shape


def _band_attention_a(q, k_slots, v_slots, slot, dil, side=()):
    b, t, nh, e = q.shape
    n_slots = k_slots.shape[0]
    tl, lc = _band_chunks(t, dil)
    nc = tl // lc
    halo = nc > 1
    off = 0 if tl == BAND else BAND
    cpb = max(1, 4 * BAND // lc) if not halo else 1
    assert dil % cpb == 0
    cls_dim = None if cpb == 1 else cpb
    view = lambda a: a.reshape(b, tl, dil, nh, e)
    view_kv = lambda a: a.reshape(n_slots, b, tl, dil, nh, e)
    chunk = pl.BlockSpec((None, lc, cls_dim, nh, e), lambda bi, r, c: (bi, c, r, 0, 0))
    chunk_kv = pl.BlockSpec((None, None, lc, cls_dim, nh, e), lambda bi, r, c: (slot, bi, c, r, 0, 0))
    halo_kv = pl.BlockSpec((None, None, BAND, None, nh, e),
                           lambda bi, r, c: (slot, bi, jnp.maximum(c * (lc // BAND) - 1, 0), r, 0, 0))
    kv_specs = [halo_kv, chunk_kv] if halo else [chunk_kv]
    kv_args = lambda a: [view_kv(a)] * len(kv_specs)
    grid = (b, dil // cpb, nc)
    side_in, side_out, side_shape = _side_cast_specs(side, grid)
    o, lse, *cast = pl.pallas_call(
        functools.partial(_band_kernel, n_kv=nh, lc=lc, dil=dil, sink=False, native=True, halo=halo, cpb=cpb,
                          n_side=len(side)),
        grid=grid,
        in_specs=[chunk] + kv_specs + kv_specs + side_in,
        out_specs=[chunk, pl.BlockSpec((None, lc, cpb * LANES), lambda bi, r, c: (bi, c, r))] + side_out,
        out_shape=[jax.ShapeDtypeStruct((b, tl, dil, nh, e), F32),
                   jax.ShapeDtypeStruct((b, tl, dil * LANES), F32)] + side_shape,
        scratch_shapes=_band_scratch(nh, lc, off),
        compiler_params=_params("arbitrary", "arbitrary", "arbitrary"),
        name=f"band_d{dil}",
    )(view(q), *kv_args(k_slots), *kv_args(v_slots), *[w for w, _ in side])
    return o.reshape(b, t, nh, e), lse.reshape(b, t, LANES), cast


def _band_attention_b(q, k, v, sinks, side=()):
    b, t, wq = q.shape
    wk = k.shape[2]
    tl, lc = _band_chunks(t, 1)
    nc = tl // lc
    halo = nc > 1
    off = 0 if tl == BAND else BAND
    chunk = lambda w: pl.BlockSpec((None, lc, w), lambda bi, r, c: (bi, c, 0))
    halo_kv = pl.BlockSpec((None, BAND, wk), lambda bi, r, c: (bi, jnp.maximum(c * (lc // BAND) - 1, 0), 0))
    kv_specs = [halo_kv, chunk(wk)] if halo else [chunk(wk)]
    n_kv = wk // HEAD_DIM
    grid = (b, 1, nc)
    side_in, side_out, side_shape = _side_cast_specs(side, grid)
    o, *cast = pl.pallas_call(
        functools.partial(_band_kernel, n_kv=n_kv, lc=lc, dil=1, sink=True, native=False, halo=halo,
                          n_side=len(side)),
        grid=grid,
        in_specs=[pl.BlockSpec(memory_space=pltpu.SMEM), chunk(wq)] + kv_specs + kv_specs + side_in,
        out_specs=[chunk(wq)] + side_out,
        out_shape=[jax.ShapeDtypeStruct((b, t, wq), BF16)] + side_shape,
        scratch_shapes=_band_scratch(n_kv, lc, off),
        compiler_params=_params("arbitrary", "arbitrary", "arbitrary"),
        name="band_sink",
    )(sinks, q, *([k] * len(kv_specs)), *([v] * len(kv_specs)), *[w for w, _ in side])
    return o, cast


def _sample_key_table(past, st, n_far, n_near, n_cols, dilations, heads_per_key=1):
    shape = (N_HEADS * SUBLANES, n_cols)
    rows = lax.broadcasted_iota(jnp.int32, shape, 0)
    col = lax.broadcasted_iota(jnp.int32, shape, 1)
    j = col // heads_per_key
    far = (j // st) * MAX_DIL + j % st
    near = past - n_near + (j - n_far)
    n = jnp.where(j < n_far, far, near)
    dist = past + (rows & (SUBLANES - 1)) - n
    cnt = jnp.zeros(shape, F32)
    for dil in dilations:
        ok = (dist >= 0) & (dist <= BAND * dil) & ((dist & (dil - 1)) == 0)
        cnt = cnt + ok.astype(F32)
    if heads_per_key > 1:
        cnt = jnp.where(col % heads_per_key == rows >> 3, cnt, 0.0)
    slope2 = jnp.exp2(-((rows >> 3) + 1).astype(F32)) * LOG2E
    return cnt, jnp.where(cnt > 0.0, -slope2 * dist.astype(F32), NEG)


def _sample_softmax(q, kall, vall, cnt, bias):
    nt = (((1,), (1,)), ((), ()))
    group = N_HEADS // kall.shape[0]
    s = jnp.concatenate([lax.dot_general(q[h], kall[h // group], nt, preferred_element_type=F32)
                         for h in range(N_HEADS)], axis=0)
    s = s + bias
    m = jnp.max(s, axis=-1, keepdims=True)
    p = cnt * jnp.exp2(s - m)
    l = jnp.sum(p, axis=-1, keepdims=True)
    o = jnp.concatenate([jnp.dot(p[h * SUBLANES:(h + 1) * SUBLANES].astype(BF16), vall[h // group],
                                 preferred_element_type=F32) for h in range(N_HEADS)], axis=0)
    return m, l, o


def _sample_kernel(sink_ref, qa_ref, kan_ref, van_ref, kfar_ref, knear_ref, vfar_ref, vnear_ref,
                   qb_ref, kbn_ref, vbn_ref, cbk_ref, cbv_ref, ya_ref, yb_ref,
                   kall_a, vall_a, kall_b, vall_b, bias_a, cnt_b, bias_b, *, past_a, past_b, st):
    n_far = kfar_ref.shape[1] * kfar_ref.shape[2]
    n_near = knear_ref.shape[1]
    n_new = kan_ref.shape[1]
    used_a = n_far + n_near + n_new
    used_b = past_b + n_new

    @pl.when(pl.program_id(0) == 0)
    def _():
        for buf, used in ((kall_a, used_a * N_HEADS), (vall_a, used_a * N_HEADS)):
            if buf.shape[0] > used:
                buf[used:, :] = jnp.zeros((buf.shape[0] - used, buf.shape[1]), BF16)
        for buf in (kall_b, vall_b):
            buf[:, used_b:, :] = jnp.zeros((buf.shape[0], buf.shape[1] - used_b, buf.shape[2]), BF16)
        ca, ba = _sample_key_table(past_a, st, n_far, n_near, kall_a.shape[0], DILATIONS, N_HEADS)
        bias_a[...] = ba + jnp.log2(jnp.maximum(ca, 1.0))
        cb, bb = _sample_key_table(past_b, st, 0, past_b, kall_b.shape[1], (1,))
        cnt_b[...] = cb
        bias_b[...] = bb

    rows2d = lambda x: x.reshape(-1, HEAD_DIM).astype(BF16)
    for bi in range(qa_ref.shape[0]):
        for src_far, src_near, src_new, dst in ((kfar_ref, knear_ref, kan_ref, kall_a),
                                                (vfar_ref, vnear_ref, van_ref, vall_a)):
            dst[0:n_far * N_HEADS, :] = rows2d(src_far[bi])
            dst[n_far * N_HEADS:(n_far + n_near) * N_HEADS, :] = rows2d(src_near[bi])
            dst[(n_far + n_near) * N_HEADS:used_a * N_HEADS, :] = rows2d(src_new[bi])
        for src_old, src_new, dst in ((cbk_ref, kbn_ref, kall_b), (cbv_ref, vbn_ref, vall_b)):
            for h in range(N_KV_B):
                dst[h, 0:past_b, :] = src_old[bi, :, h, :].astype(BF16)
                dst[h, past_b:used_b, :] = src_new[bi, :, _head_cols(h)].astype(BF16)

        qa = _native_to_heads(qa_ref[bi]).reshape(N_HEADS * SUBLANES, HEAD_DIM).astype(BF16)
        s = lax.dot_general(qa, kall_a[...], (((1,), (1,)), ((), ())), preferred_element_type=F32) + bias_a[...]
        m = jnp.max(s, axis=-1, keepdims=True)
        p = jnp.exp2(s - m)
        l = jnp.sum(p, axis=-1, keepdims=True)
        y = jnp.dot(p.astype(BF16), vall_a[...], preferred_element_type=F32) / l
        for h in range(N_HEADS):
            ya_ref[bi, :, _head_cols(h)] = y[h * SUBLANES:(h + 1) * SUBLANES]

        qb = jnp.stack([qb_ref[bi, :, _head_cols(h)] for h in range(N_HEADS)], axis=0).astype(BF16)
        m, l, o = _sample_softmax(qb, kall_b, vall_b, cnt_b[...], bias_b[...])
        rows = lax.broadcasted_iota(jnp.int32, m.shape, 0)
        sk = jnp.zeros(m.shape, F32)
        for h in range(N_HEADS):
            sk = jnp.where(rows >> 3 == h, sink_ref[h] * LOG2E, sk)
        mm = jnp.maximum(m, sk)
        a = jnp.exp2(m - mm)
        y = o * a / (l * a + jnp.exp2(sk - mm))
        for h in range(N_HEADS):
            yb_ref[bi, :, _head_cols(h)] = y[h * SUBLANES:(h + 1) * SUBLANES]


def _sample_attention(layer, st, sinks, qa, ka_new, va_new, cache_ak, cache_av, qb, kb_new, vb_new,
                      cache_bk, cache_bv):
    depth, b, past_a, nh, e = cache_ak.shape
    past_b, n_kv = cache_bk.shape[2], cache_bk.shape[3]
    n_new = ka_new.shape[1]
    wq = nh * e
    near_rows = BAND * DILATIONS[-2]
    assert past_a == BAND * MAX_DIL and st <= DILATIONS[-2] and past_b >= BAND and st <= SUBLANES
    assert past_b % BF16_ROWS == 0 and n_new % BF16_ROWS == 0
    n_far_blocks = (past_a - near_rows) // MAX_DIL
    n_cols_a = pl.cdiv((n_far_blocks * st + near_rows + n_new) * nh, LANES) * LANES
    n_keys_b = pl.cdiv(past_b + n_new, LANES) * LANES
    assert n_far_blocks % BF16_ROWS == 0 and near_rows % BF16_ROWS == 0 and past_a % near_rows == 0
    gb = 2 if b % 2 == 0 else 1
    view = lambda c: c.reshape(depth, b, past_a // MAX_DIL, MAX_DIL, nh, e)
    far = pl.BlockSpec((None, gb, n_far_blocks, st, nh, e), lambda i: (layer, i, 0, 0, 0, 0))
    near = pl.BlockSpec((None, gb, near_rows, nh, e), lambda i: (layer, i, past_a // near_rows - 1, 0, 0))
    per_b = lambda *shape: pl.BlockSpec((gb,) + shape, lambda i: (i,) + (0,) * len(shape))
    cache_b = pl.BlockSpec((None, gb, past_b, n_kv, e), lambda i: (layer, i, 0, 0, 0))
    table = lambda n: pltpu.VMEM((N_HEADS * SUBLANES, n), F32)
    return pl.pallas_call(
        functools.partial(_sample_kernel, past_a=past_a, past_b=past_b, st=st),
        grid=(b // gb,),
        in_specs=[pl.BlockSpec(memory_space=pltpu.SMEM),
                  per_b(SUBLANES, nh, e), per_b(n_new, nh, e), per_b(n_new, nh, e), far, near, far, near,
                  per_b(SUBLANES, wq), per_b(n_new, n_kv * e), per_b(n_new, n_kv * e), cache_b, cache_b],
        out_specs=[per_b(SUBLANES, wq), per_b(SUBLANES, wq)],
        out_shape=[jax.ShapeDtypeStruct((b, SUBLANES, wq), F32), jax.ShapeDtypeStruct((b, SUBLANES, wq), F32)],
        scratch_shapes=[pltpu.VMEM((n_cols_a, e), BF16), pltpu.VMEM((n_cols_a, e), BF16),
                        pltpu.VMEM((n_kv, n_keys_b, e), BF16), pltpu.VMEM((n_kv, n_keys_b, e), BF16),
                        table(n_cols_a), table(n_keys_b), table(n_keys_b)],
        compiler_params=_params("arbitrary"),
        name="sample_attn",
    )(sinks, qa, ka_new, va_new, view(cache_ak), cache_ak, view(cache_av), cache_av,
      qb, kb_new, vb_new, cache_bk, cache_bv)


def _merge_kernel(*refs, n_groups, tn, pack_cols=0):
    if n_groups:
        o_refs, refs = refs[:n_groups], refs[n_groups:]
        lse_refs, refs = refs[:n_groups], refs[n_groups:]
    else:
        ya_ref, refs = refs[0], refs[1:]
    if pack_cols:
        yb_ref, ga_ref, gb_ref, wba_ref, wbb_ref, win_ref, out_ref, wpack_ref, ya_s = refs
        pad = wpack_ref.shape[1] - win_ref.shape[1]
        wpack_ref[:, :pack_cols] = win_ref[:, :pack_cols].astype(BF16)
        wpack_ref[:, pack_cols:pack_cols + pad] = jnp.zeros((wpack_ref.shape[0], pad), BF16)
        wpack_ref[:, pack_cols + pad:] = win_ref[:, pack_cols:].astype(BF16)
    else:
        yb_ref, ga_ref, gb_ref, wba_ref, wbb_ref, out_ref, ya_s = refs
    if n_groups:
        s = pl.program_id(0)

        @pl.when(s == 0)
        def _():
            ya_s[1] = jnp.zeros(ya_s.shape[1:], BF16)

        ya = ya_s[(s + 1) % 2]
        lses = [r[...] for r in lse_refs]
        os_ = [_native_to_heads(r[...]) for r in o_refs]
        nxt = s % 2
        for h in range(N_HEADS):
            e = [l[:, h:h + 1] for l in lses]
            m = functools.reduce(jnp.maximum, e)
            w = [jnp.exp2(ei - m) for ei in e]
            num = sum(wi * o[h] for wi, o in zip(w, os_))
            ya_s[nxt, :, _head_cols(h)] = (num / sum(w)).astype(BF16)
    else:
        ya = ya_ref[...].astype(BF16)
    yb = yb_ref[...].astype(BF16)
    for n in range(out_ref.shape[1] // tn):
        sl = slice(n * tn, (n + 1) * tn)
        ya_d = jnp.dot(ya, wba_ref[:, sl], preferred_element_type=F32)
        yb_d = jnp.dot(yb, wbb_ref[:, sl], preferred_element_type=F32)
        out_ref[:, sl] = (ga_ref[:, sl].astype(F32) * ya_d + gb_ref[:, sl].astype(F32) * yb_d).astype(BF16)


def _outproj_kernel(mg_ref, x_ref, wo_ref, out_ref, *, tn):
    for n in range(out_ref.shape[1] // tn):
        sl = slice(n * tn, (n + 1) * tn)
        out_ref[:, sl] = x_ref[:, sl] + jnp.dot(mg_ref[...], wo_ref[:, sl], preferred_element_type=F32)


def _post(ya_parts, yb, gates, x, wba, wbb, wo, tm_merge, tm_out, pack=None):
    m, d = x.shape
    wa = wba.shape[0]
    whole = lambda a: pl.BlockSpec(a.shape, lambda i: (0, 0), pipeline_mode=pl.Buffered(1))
    n_blocks = m // tm_merge
    if isinstance(ya_parts, tuple):
        os_, lses = ya_parts
        n_groups = len(os_)
        ya_args = list(os_) + list(lses)
        steps = n_blocks + 1
        ahead = lambda s: jnp.minimum(s, n_blocks - 1)
        cur = lambda s: jnp.maximum(s - 1, 0)
        ya_specs = ([pl.BlockSpec((tm_merge, N_HEADS, HEAD_DIM), lambda s: (ahead(s), 0, 0))] * n_groups
                    + [pl.BlockSpec((tm_merge, LANES), lambda s: (ahead(s), 0))] * n_groups)
        scratch = [pltpu.VMEM((2, tm_merge, wa), BF16)]
        sem = "arbitrary"
    else:
        n_groups = 0
        ya_args = [ya_parts]
        steps = n_blocks
        cur = lambda s: s
        ya_specs = [pl.BlockSpec((tm_merge, wa), lambda s: (s, 0))]
        scratch = [pltpu.VMEM((SUBLANES, LANES), BF16)]
        sem = "parallel"
    rows = lambda w, col=0: pl.BlockSpec((tm_merge, w), lambda s: (cur(s), col))
    pack_in, pack_out, pack_shape, pack_args, pack_cols = [], [], [], [], 0
    if pack is not None:
        w_in, layer = pack
        _, w_rows, w_cols = w_in.shape
        pack_cols = 4 * wa + 2 * N_KV_B * HEAD_DIM
        packed_cols = w_cols + TN_IN - 2 * N_KV_B * HEAD_DIM
        assert w_rows % n_blocks == 0 and (w_rows // n_blocks) % BF16_ROWS == 0 and pack_cols % LANES == 0
        pack_in = [pl.BlockSpec((None, w_rows // n_blocks, w_cols), lambda s: (layer, cur(s), 0))]
        pack_out = [pl.BlockSpec((w_rows // n_blocks, packed_cols), lambda s: (cur(s), 0))]
        pack_shape = [jax.ShapeDtypeStruct((w_rows, packed_cols), BF16)]
        pack_args = [w_in]
    merged, *packed = pl.pallas_call(
        functools.partial(_merge_kernel, n_groups=n_groups, tn=512, pack_cols=pack_cols),
        grid=(steps,),
        in_specs=ya_specs + [rows(wa), rows(d, 0), rows(d, 1), whole(wba), whole(wbb)] + pack_in,
        out_specs=[rows(d)] + pack_out,
        out_shape=[jax.ShapeDtypeStruct((m, d), BF16)] + pack_shape,
        scratch_shapes=scratch,
        compiler_params=_params(sem),
        name="merge_branches",
    )(*ya_args, yb, gates, gates, wba, wbb, *pack_args)
    rows_o = pl.BlockSpec((tm_out, d), lambda i: (i, 0))
    x1 = pl.pallas_call(
        functools.partial(_outproj_kernel, tn=512),
        grid=(m // tm_out,),
        in_specs=[rows_o, rows_o, whole(wo)],
        out_specs=rows_o,
        out_shape=jax.ShapeDtypeStruct((m, d), F32),
        compiler_params=_params("parallel"),
        name="out_proj",
    )(merged, x, wo)
    return (x1, packed[0]) if pack is not None else x1


def _ffn_kernel(x_ref, g_ref, wg_ref, wu_ref, wd_ref, out_ref, h_ref):
    f = pl.program_id(1)

    @pl.when(f == 0)
    def _():
        x = x_ref[...]
        h_ref[...] = _rms_rows(x, g_ref[...]).astype(BF16)
        out_ref[...] = x

    h = h_ref[...]
    gate = jnp.dot(h, wg_ref[...], preferred_element_type=F32)
    up = jnp.dot(h, wu_ref[...], preferred_element_type=F32)
    u = (jax.nn.silu(gate) * up).astype(BF16)
    out_ref[...] += jnp.dot(u, wd_ref[...], preferred_element_type=F32)


def _ffn(x, g, wg, wu, wd, tm, tf):
    m, d = x.shape
    dff = wg.shape[1]
    assert m % tm == 0 and dff % tf == 0
    return pl.pallas_call(
        _ffn_kernel,
        grid=(m // tm, dff // tf),
        in_specs=[pl.BlockSpec((tm, d), lambda i, f: (i, 0)), pl.BlockSpec((1, d), lambda i, f: (0, 0)),
                  pl.BlockSpec((d, tf), lambda i, f: (0, f)), pl.BlockSpec((d, tf), lambda i, f: (0, f)),
                  pl.BlockSpec((tf, d), lambda i, f: (f, 0))],
        out_specs=pl.BlockSpec((tm, d), lambda i, f: (i, 0)),
        out_shape=jax.ShapeDtypeStruct((m, d), F32),
        scratch_shapes=[pltpu.VMEM((tm, d), BF16)],
        compiler_params=_params("parallel", "arbitrary"),
        name="ffn",
    )(x, g, wg, wu, wd)


def _pad_rows(a, n):
    return jnp.pad(a, ((0, 0), (0, n - a.shape[1])) + ((0, 0),) * (a.ndim - 2))


def kernel(x_prompt, x_sample, cache_a_k, cache_a_v, cache_b_k, cache_b_v, norm_mix, w_in, qnorm_a, knorm_a,
           qnorm_b, knorm_b, sinks_b, w_branch_a, w_branch_b, w_out, norm_ffn, w_ffn_gate, w_ffn_up, w_ffn_down):
    depth = w_in.shape[0]
    b, t, d = x_prompt.shape
    sb, st, _ = x_sample.shape
    wa = N_HEADS * HEAD_DIM
    keep_b = min(BAND, t)
    xp = x_prompt.reshape(b * t, d)
    xs = x_sample.reshape(sb * st, d)
    tm_p = 512
    tm_s = sb * st
    row = lambda v: v.reshape(1, -1)
    outs = [[] for _ in range(6)]
    k_slots = v_slots = None
    for l in range(depth):
        w_in_l = _pack_w_in(w_in[l], d) if l == 0 else w_in_next
        norms = (row(qnorm_a[l]), row(knorm_a[l]), row(qnorm_b[l]), row(knorm_b[l]))

        qa, k_slots, v_slots, qb, kb, vb, gates = _inproj(xp, row(norm_mix[l]), w_in_l, *norms, tm=tm_p,
                                                          prev=(k_slots, v_slots) if l else None)
        seq = lambda a: a.reshape((b, t) + a.shape[1:])
        per_seq = lambda a: a.reshape((l + 1, b, t) + a.shape[2:])
        ffn_w = (w_ffn_gate, w_ffn_up, w_ffn_down)
        assert len(DILATIONS) == len(ffn_w)
        parts = [_band_attention_a(seq(qa), per_seq(k_slots), per_seq(v_slots), l, dil, side=[(w, l)])
                 for dil, w in zip(DILATIONS, ffn_w)]
        os_ = [o.reshape(b * t, N_HEADS, HEAD_DIM) for o, _, _ in parts]
        lses = [s.reshape(b * t, LANES) for _, s, _ in parts]
        (wg,), (wu,), (wd,) = [cast for _, _, cast in parts]
        yb, (wo, wba, wbb) = _band_attention_b(seq(qb), seq(kb), seq(vb), sinks_b[l],
                                               side=[(w_out, l), (w_branch_a, l), (w_branch_b, l)])
        yb = yb.reshape(b * t, wa)
        x1 = _post((os_, lses), yb, gates, xp, wba, wbb, wo, tm_merge=tm_p, tm_out=2 * tm_p,
                   pack=(w_in, l + 1) if l + 1 < depth else None)
        if l + 1 < depth:
            x1, w_in_next = x1
        xp = _ffn(x1, row(norm_ffn[l]), wg, wu, wd, tm=2 * tm_p, tf=512)
        outs[0].append(kb.reshape(b, t, N_KV_B, HEAD_DIM)[:, t - keep_b:])
        outs[1].append(vb.reshape(b, t, N_KV_B, HEAD_DIM)[:, t - keep_b:])

        qa, ka, va, qb, kb, vb, gates = _inproj(xs, row(norm_mix[l]), w_in_l, *norms, tm=tm_s)
        ka, va = ka[0], va[0]
        tok = lambda a: a.reshape((sb, st) + a.shape[1:])
        ya, yb = _sample_attention(
            l, st, sinks_b[l],
            _pad_rows(tok(qa), SUBLANES), _pad_rows(tok(ka), BF16_ROWS), _pad_rows(tok(va), BF16_ROWS),
            cache_a_k, cache_a_v,
            _pad_rows(tok(qb).astype(F32), SUBLANES), _pad_rows(tok(kb), BF16_ROWS), _pad_rows(tok(vb), BF16_ROWS),
            cache_b_k, cache_b_v)
        ya = ya[:, :st].reshape(sb * st, wa)
        yb = yb[:, :st].reshape(sb * st, wa)
        x1 = _post(ya, yb, gates, xs, wba, wbb, wo, tm_merge=tm_s, tm_out=tm_s)
        xs = _ffn(x1, row(norm_ffn[l]), wg, wu, wd, tm=tm_s, tf=512)
        outs[2].append(tok(ka))
        outs[3].append(tok(va))
        outs[4].append(kb.reshape(sb, st, N_KV_B, HEAD_DIM))
        outs[5].append(vb.reshape(sb, st, N_KV_B, HEAD_DIM))
    cache_a = tuple(a.reshape(depth, b, t, N_HEADS, HEAD_DIM) for a in (k_slots, v_slots))
    return (xp.reshape(b, t, d), xs.reshape(sb, st, d)) + cache_a + tuple(jnp.stack(o) for o in outs)
```
